```python
import math
import jax, jax.numpy as jnp
from jax import lax
import numpy as np

D_MODEL = 1024
BATCH = 1
SEQ = 16384
DEPTH = 1
DEC_BATCH = 4
DEC_SEQ = 4096
PAST_LEN = 128

GRID_W = 64
N_Q_HEADS = 8
N_KV_HEADS = 2
HEAD_DIM = 64
ATTN_WIDTH = N_Q_HEADS * HEAD_DIM
KV_WIDTH = N_KV_HEADS * HEAD_DIM
Q_BLOCK = 128
ROPE_THETA = 10000.0
GLA_HEADS = 4
GLA_DK = 64
GLA_DV = 128
GLA_KEY_WIDTH = GLA_HEADS * GLA_DK
GLA_WIDTH = GLA_HEADS * GLA_DV
GATE_RANK = 16
GATE_NORMALIZER = 16.0
CHUNK = 64
MIX_WIDTH = ATTN_WIDTH + GLA_WIDTH
SPLIT_SIZES = (ATTN_WIDTH, KV_WIDTH, KV_WIDTH, ATTN_WIDTH,
               GLA_KEY_WIDTH, GLA_KEY_WIDTH, GLA_WIDTH, GLA_WIDTH,
               GATE_RANK, GATE_RANK)
IN_WIDTH = 2848
PLE_DIM = 256
EPS = 1e-6

kernel_name = "hymba_gqa_axialrope_bigla_encoder"


def rms_norm(x, w):
    xf = x.astype(jnp.float32)
    y = xf * lax.rsqrt(jnp.mean(xf * xf, axis=-1, keepdims=True) + EPS)
    return (y * w.astype(jnp.float32)).astype(x.dtype)


def axial_rope_tables(T):
    rows = T // GRID_W
    row_idx = jnp.repeat(jnp.arange(rows, dtype=jnp.float32), GRID_W)
    col_idx = jnp.tile(jnp.arange(GRID_W, dtype=jnp.float32), rows)
    half = HEAD_DIM // 2
    inv_freq = ROPE_THETA ** (-jnp.arange(0, half, 2, dtype=jnp.float32) / half)
    ang_r = row_idx[:, None] * inv_freq[None, :]
    ang_c = col_idx[:, None] * inv_freq[None, :]
    return jnp.cos(ang_r), jnp.sin(ang_r), jnp.cos(ang_c), jnp.sin(ang_c)


def apply_axial_rope(x, tables):
    cr, sr, cc, sc = (t[None, :, None, :] for t in tables)
    x_r1, x_r2, x_c1, x_c2 = jnp.split(x, 4, axis=-1)
    return jnp.concatenate([x_r1 * cr - x_r2 * sr, x_r2 * cr + x_r1 * sr,
                            x_c1 * cc - x_c2 * sc, x_c2 * cc + x_c1 * sc], axis=-1)


def block_attention(q, k, v):
    B, T, _, D = q.shape
    G = N_Q_HEADS // N_KV_HEADS
    nb = T // Q_BLOCK
    scale = D ** -0.5
    qb = q.reshape(B, nb, Q_BLOCK, N_KV_HEADS, G, D).transpose(1, 0, 3, 4, 2, 5)

    def one_block(q_blk):
        s = jnp.einsum('bkgqd,btkd->bkgqt', q_blk, k).astype(jnp.float32) * scale
        p = jax.nn.softmax(s, axis=-1)
        return jnp.einsum('bkgqt,btkd->bkgqd', p.astype(v.dtype), v)

    o = lax.map(one_block, qb)
    return o.transpose(1, 0, 4, 2, 3, 5).reshape(B, T, N_Q_HEADS * D)


def gla_chunked(q, k, v, g, include_diag):
    B, T, H, DK = q.shape
    DV = v.shape[-1]
    N = T // CHUNK

    def to_chunks(a):
        return a.astype(jnp.float32).reshape(B, N, CHUNK, H, a.shape[-1]).transpose(0, 3, 1, 2, 4)

    q, k, v, g = to_chunks(q) * (DK ** -0.5), to_chunks(k), to_chunks(v), to_chunks(g)
    b = jnp.cumsum(g, axis=3)
    b_last = b[:, :, :, -1:, :]
    q_dec = q * jnp.exp(b)
    a = jnp.einsum('bhncd,bhnsd->bhncs', q_dec, k * jnp.exp(-b))
    mask = jnp.tril(jnp.ones((CHUNK, CHUNK), dtype=bool), k=0 if include_diag else -1)
    a = jnp.where(mask, a, 0.0)
    o_intra = jnp.einsum('bhncs,bhnse->bhnce', a, v)
    kv = jnp.einsum('bhncd,bhnce->bhnde', k * jnp.exp(b_last - b), v)
    decay = jnp.exp(b_last[:, :, :, 0, :])

    def step(state, inp):
        dec, kv_c = inp
        return dec[..., None] * state + kv_c, state

    _, s_prev = lax.scan(step, jnp.zeros((B, H, DK, DV), jnp.float32),
                         (jnp.moveaxis(decay, 2, 0), jnp.moveaxis(kv, 2, 0)))
    s_prev = jnp.moveaxis(s_prev, 0, 2)
    o_inter = jnp.einsum('bhncd,bhnde->bhnce', q_dec, s_prev)
    return (o_intra + o_inter).transpose(0, 2, 3, 1, 4).reshape(B, T, H, DV)


def hybrid_layer(h, p, mix_norm, w_in, q_norm, k_norm, w_gate_up_fwd, b_gate_fwd,
                 w_gate_up_bwd, b_gate_bwd, gla_norm, w_out, ple_norm, w_ple_gate, w_ple_proj):
    B, T, _ = h.shape
    xn = rms_norm(h, mix_norm)
    z = xn @ w_in
    idx, acc = [], 0
    for s in SPLIT_SIZES[:-1]:
        acc += s
        idx.append(acc)
    (a_q, a_k, a_v, a_gate, l_q, l_k, l_v, l_gate, lr_f, lr_b) = jnp.split(z, idx, axis=-1)

    tables = axial_rope_tables(T)
    q = rms_norm(a_q.reshape(B, T, N_Q_HEADS, HEAD_DIM).astype(jnp.float32), q_norm)
    k = rms_norm(a_k.reshape(B, T, N_KV_HEADS, HEAD_DIM).astype(jnp.float32), k_norm)
    q = apply_axial_rope(q, tables)
    k = apply_axial_rope(k, tables)
    v = a_v.reshape(B, T, N_KV_HEADS, HEAD_DIM).astype(jnp.float32)
    o_attn = block_attention(q, k, v).astype(h.dtype)

    gq = l_q.reshape(B, T, GLA_HEADS, GLA_DK)
    gk = l_k.reshape(B, T, GLA_HEADS, GLA_DK)
    gv = l_v.reshape(B, T, GLA_HEADS, GLA_DV)
    g_f = jax.nn.log_sigmoid((lr_f @ w_gate_up_fwd + b_gate_fwd).astype(jnp.float32)) / GATE_NORMALIZER
    g_b = jax.nn.log_sigmoid((lr_b @ w_gate_up_bwd + b_gate_bwd).astype(jnp.float32)) / GATE_NORMALIZER
    g_f = g_f.reshape(B, T, GLA_HEADS, GLA_DK)
    g_b = g_b.reshape(B, T, GLA_HEADS, GLA_DK)
    o_fwd = gla_chunked(gq, gk, gv, g_f, include_diag=True)
    flip = lambda a: jnp.flip(a, axis=1)
    o_bwd = flip(gla_chunked(flip(gq), flip(gk), flip(gv), flip(g_b), include_diag=False))
    o_gla = rms_norm(o_fwd + o_bwd, gla_norm).reshape(B, T, GLA_WIDTH).astype(h.dtype)

    mixed = jnp.concatenate([o_attn * jax.nn.silu(a_gate), o_gla * jax.nn.silu(l_gate)], axis=-1)
    h = h + mixed @ w_out

    gate = jax.nn.sigmoid(rms_norm(h, ple_norm) @ w_ple_gate)
    return h + gate * (p @ w_ple_proj)


def setup_inputs(seed: int = 0) -> dict:
    key = jax.random.key(seed)
    ks = jax.random.split(key, 20)
    nrm = lambda k_, shape, s: jax.random.normal(k_, shape, jnp.float32) * s
    gain = lambda k_, shape: 1.0 + 0.05 * jax.random.normal(k_, shape, jnp.float32)
    return {
        "x_prompt": nrm(ks[0], (BATCH, SEQ, D_MODEL), 1.0),
        "x_sample": nrm(ks[1], (DEC_BATCH, DEC_SEQ, D_MODEL), 1.0),
        "p_prompt": nrm(ks[2], (DEPTH, BATCH, SEQ, PLE_DIM), 1.0),
        "p_sample": nrm(ks[3], (DEPTH, DEC_BATCH, DEC_SEQ, PLE_DIM), 1.0),
        "mix_norm": gain(ks[4], (DEPTH, D_MODEL)),
        "w_in": nrm(ks[5], (DEPTH, D_MODEL, IN_WIDTH), D_MODEL ** -0.5),
        "q_norm": gain(ks[6], (DEPTH, HEAD_DIM)),
        "k_norm": gain(ks[7], (DEPTH, HEAD_DIM)),
        "w_gate_up_fwd": nrm(ks[8], (DEPTH, GATE_RANK, GLA_KEY_WIDTH), GATE_RANK ** -0.5),
        "b_gate_fwd": nrm(ks[9], (DEPTH, GLA_KEY_WIDTH), 0.1),
        "w_gate_up_bwd": nrm(ks[10], (DEPTH, GATE_RANK, GLA_KEY_WIDTH), GATE_RANK ** -0.5),
        "b_gate_bwd": nrm(ks[11], (DEPTH, GLA_KEY_WIDTH), 0.1),
        "gla_norm": gain(ks[12], (DEPTH, GLA_DV)),
        "w_out": nrm(ks[13], (DEPTH, MIX_WIDTH, D_MODEL), MIX_WIDTH ** -0.5),
        "ple_norm": gain(ks[14], (DEPTH, D_MODEL)),
        "w_ple_gate": nrm(ks[15], (DEPTH, D_MODEL, D_MODEL), D_MODEL ** -0.5),
        "w_ple_proj": nrm(ks[16], (DEPTH, PLE_DIM, D_MODEL), PLE_DIM ** -0.5),
        "final_norm": gain(ks[17], (D_MODEL,)),
    }


def reference(x_prompt, x_sample, p_prompt, p_sample, mix_norm, w_in, q_norm, k_norm,
              w_gate_up_fwd, b_gate_fwd, w_gate_up_bwd, b_gate_bwd, gla_norm, w_out,
              ple_norm, w_ple_gate, w_ple_proj, final_norm):
    hp, hs = x_prompt, x_sample
    for i in range(DEPTH):
        lw = (mix_norm[i], w_in[i], q_norm[i], k_norm[i], w_gate_up_fwd[i], b_gate_fwd[i],
              w_gate_up_bwd[i], b_gate_bwd[i], gla_norm[i], w_out[i], ple_norm[i],
              w_ple_gate[i], w_ple_proj[i])
        hp = hybrid_layer(hp, p_prompt[i], *lw)
        hs = hybrid_layer(hs, p_sample[i], *lw)
    y_prompt = rms_norm(hp, final_norm)
    y_sample = rms_norm(hs, final_norm)
    return (y_prompt, y_sample)
```

```python
import functools
import math

import jax
import jax.numpy as jnp
from jax import lax
from jax.experimental import pallas as pl
from jax.experimental.pallas import tpu as pltpu

F32 = jnp.float32
BF16 = jnp.bfloat16

D_MODEL = 1024
GRID_W = 64
N_Q_HEADS = 8
N_KV_HEADS = 2
HEAD_DIM = 64
Q_PER_KV = N_Q_HEADS // N_KV_HEADS
ATTN_WIDTH = N_Q_HEADS * HEAD_DIM
KV_WIDTH = N_KV_HEADS * HEAD_DIM
ROPE_THETA = 10000.0
GLA_HEADS = 4
GLA_DK = 64
GLA_DV = 128
GLA_KEY_WIDTH = GLA_HEADS * GLA_DK
GLA_WIDTH = GLA_HEADS * GLA_DV
GATE_RANK = 16
GATE_NORMALIZER = 16.0
CHUNK = 64
PLE_DIM = 256
EPS = 1e-6
LOG2E = math.log2(math.e)

VMEM_LIMIT_BYTES = 56 * 1024 * 1024
BF16_SUBLANES = 16
V_ROWS = HEAD_DIM + BF16_SUBLANES
NEG_BIG = -1e30

NT_DIMS = (((1,), (1,)), ((), ()))


def _dot(a, b):
    return jnp.dot(a, b, preferred_element_type=F32)


def _silu(x):
    return x * jax.nn.sigmoid(x)


def _inproj_kernel(x_ref, mixw_ref, wrow_ref, wt_ref, nw_ref, rope_ref, wup_ref, bup_ref,
                   qT_ref, k_ref, vT_ref, ag_ref, lq_ref, lv_ref, lg_ref, lkT_ref, gT_ref):
    x = x_ref[...]
    xn = x * lax.rsqrt(jnp.mean(x * x, axis=-1, keepdims=True) + EPS) * mixw_ref[...]
    xn = xn.astype(BF16)

    zr = _dot(xn, wrow_ref[...])
    ag_ref[...] = _silu(zr[:, 0:ATTN_WIDTH])
    lq_ref[...] = zr[:, 512:768]
    lv_ref[...] = zr[:, 768:1280]
    lg_ref[...] = _silu(zr[:, 1280:1792])

    zt = lax.dot_general(wt_ref[...], xn, NT_DIMS, preferred_element_type=F32)
    tm = zt.shape[1]
    cr, sr, cc, sc = rope_ref[0], rope_ref[1], rope_ref[2], rope_ref[3]

    def norm_rope(zh, w):
        y = zh * lax.rsqrt(jnp.mean(zh * zh, axis=0, keepdims=True) + EPS) * w
        r1, r2, c1, c2 = y[0:16], y[16:32], y[32:48], y[48:64]
        return jnp.concatenate([r1 * cr - r2 * sr, r2 * cr + r1 * sr,
                                c1 * cc - c2 * sc, c2 * cc + c1 * sc], axis=0)

    for h in range(N_Q_HEADS):
        qh = norm_rope(zt[h * HEAD_DIM:(h + 1) * HEAD_DIM], nw_ref[h])
        qT_ref[0, h * HEAD_DIM:(h + 1) * HEAD_DIM, :] = qh.astype(BF16)
    kT = jnp.concatenate(
        [norm_rope(zt[ATTN_WIDTH + h * HEAD_DIM:ATTN_WIDTH + (h + 1) * HEAD_DIM], nw_ref[N_Q_HEADS + h])
         for h in range(N_KV_HEADS)], axis=0)
    k_ref[...] = kT.T.astype(BF16)

    ones = jnp.ones((BF16_SUBLANES, tm), F32)
    v0 = ATTN_WIDTH + KV_WIDTH
    vT = jnp.concatenate([zt[v0:v0 + HEAD_DIM], ones, zt[v0 + HEAD_DIM:v0 + 2 * HEAD_DIM], ones], axis=0)
    vT_ref[0, 0] = vT.astype(BF16)

    l0 = v0 + KV_WIDTH
    lkT_ref[0] = zt[l0:l0 + GLA_KEY_WIDTH]
    lrT = zt[l0 + GLA_KEY_WIDTH:l0 + GLA_KEY_WIDTH + 2 * GATE_RANK].astype(BF16)
    gpre = _dot(wup_ref[...], lrT) + bup_ref[...]
    gT_ref[0] = jax.nn.log_sigmoid(gpre) * (1.0 / GATE_NORMALIZER)


def _inproj(x2d, B, T, tm, mixw, wrow, wt, nw, rope, wup, bup):
    nt = T // tm
    grid = (B, nt)
    tok = lambda b, i: (b * nt + i, 0)
    fm = lambda b, i: (b, 0, i)
    const2 = lambda b, i: (0, 0)
    const3 = lambda b, i: (0, 0, 0)
    out_shape = (
        jax.ShapeDtypeStruct((B, ATTN_WIDTH, T), BF16),
        jax.ShapeDtypeStruct((B * T, KV_WIDTH), BF16),
        jax.ShapeDtypeStruct((B, nt, N_KV_HEADS * V_ROWS, tm), BF16),
        jax.ShapeDtypeStruct((B * T, ATTN_WIDTH), F32),
        jax.ShapeDtypeStruct((B * T, GLA_KEY_WIDTH), F32),
        jax.ShapeDtypeStruct((B * T, GLA_WIDTH), F32),
        jax.ShapeDtypeStruct((B * T, GLA_WIDTH), F32),
        jax.ShapeDtypeStruct((B, GLA_KEY_WIDTH, T), F32),
        jax.ShapeDtypeStruct((B, 2 * GLA_KEY_WIDTH, T), F32),
    )
    out_specs = (
        pl.BlockSpec((1, ATTN_WIDTH, tm), fm),
        pl.BlockSpec((tm, KV_WIDTH), tok),
        pl.BlockSpec((1, 1, N_KV_HEADS * V_ROWS, tm), lambda b, i: (b, i, 0, 0)),
        pl.BlockSpec((tm, ATTN_WIDTH), tok),
        pl.BlockSpec((tm, GLA_KEY_WIDTH), tok),
        pl.BlockSpec((tm, GLA_WIDTH), tok),
        pl.BlockSpec((tm, GLA_WIDTH), tok),
        pl.BlockSpec((1, GLA_KEY_WIDTH, tm), fm),
        pl.BlockSpec((1, 2 * GLA_KEY_WIDTH, tm), fm),
    )
    in_specs = [
        pl.BlockSpec((tm, D_MODEL), tok),
        pl.BlockSpec(mixw.shape, const2),
        pl.BlockSpec(wrow.shape, const2),
        pl.BlockSpec(wt.shape, const2),
        pl.BlockSpec(nw.shape, const3),
        pl.BlockSpec((4, HEAD_DIM // 4, tm), lambda b, i: (0, 0, i)),
        pl.BlockSpec(wup.shape, const2),
        pl.BlockSpec(bup.shape, const2),
    ]
    return pl.pallas_call(
        _inproj_kernel, grid=grid, in_specs=in_specs, out_specs=out_specs, out_shape=out_shape,
        name="inproj",
        compiler_params=pltpu.CompilerParams(
            dimension_semantics=("arbitrary", "arbitrary"), vmem_limit_bytes=VMEM_LIMIT_BYTES),
    )(x2d, mixw, wrow, wt, nw, rope, wup, bup)


def _attn_kernel(qT_ref, k_ref, vT_ref, gate_ref, o_ref, m_ref, acc_ref, *, tq, kc, nchunks):
    g = pl.program_id(1)
    nq = Q_PER_KV * tq
    qcat = jnp.concatenate([qT_ref[0, h * HEAD_DIM:(h + 1) * HEAD_DIM, :] for h in range(Q_PER_KV)], axis=1)
    zeros = jnp.zeros_like(qcat)
    w = jnp.where(g == 0, jnp.concatenate([qcat, zeros], axis=0), jnp.concatenate([zeros, qcat], axis=0))

    m_ref[...] = jnp.full((1, nq), NEG_BIG, F32)
    acc_ref[...] = jnp.zeros((V_ROWS, nq), F32)

    def body(c, carry):
        kch = k_ref[pl.ds(pl.multiple_of(c * kc, kc), kc), :]
        s = _dot(kch, w)
        m_old = m_ref[...]
        m_new = jnp.maximum(m_old, jnp.max(s, axis=0, keepdims=True))
        alpha = jnp.exp2(m_old - m_new)
        p = jnp.exp2(s - m_new).astype(BF16)
        acc_ref[...] = acc_ref[...] * alpha + _dot(vT_ref[0, c], p)
        m_ref[...] = m_new
        return carry

    lax.fori_loop(0, nchunks, body, 0)

    acc = acc_ref[...]
    o = acc[0:HEAD_DIM] / acc[HEAD_DIM:HEAD_DIM + 1]
    oT = jnp.concatenate([o[:, h * tq:(h + 1) * tq] for h in range(Q_PER_KV)], axis=0)
    o_ref[...] = (oT.T * gate_ref[...]).astype(BF16)


def _attention(qT, k, vT, gate, B, T, tq, kc):
    nq_tiles = T // tq
    nchunks = T // kc
    gw = Q_PER_KV * HEAD_DIM
    kern = functools.partial(_attn_kernel, tq=tq, kc=kc, nchunks=nchunks)
    return pl.pallas_call(
        kern,
        grid=(B, N_KV_HEADS, nq_tiles),
        in_specs=[
            pl.BlockSpec((1, gw, tq), lambda b, g, i: (b, g, i)),
            pl.BlockSpec((T, KV_WIDTH), lambda b, g, i: (b, 0)),
            pl.BlockSpec((1, nchunks, V_ROWS, kc), lambda b, g, i: (b, 0, g, 0)),
            pl.BlockSpec((tq, gw), lambda b, g, i: (b * nq_tiles + i, g)),
        ],
        out_specs=pl.BlockSpec((tq, gw), lambda b, g, i: (b * nq_tiles + i, g)),
        out_shape=jax.ShapeDtypeStruct((B * T, ATTN_WIDTH), BF16),
        scratch_shapes=[pltpu.VMEM((1, Q_PER_KV * tq), F32), pltpu.VMEM((V_ROWS, Q_PER_KV * tq), F32)],
        name="attention",
        compiler_params=pltpu.CompilerParams(
            dimension_semantics=("arbitrary", "arbitrary", "arbitrary"), vmem_limit_bytes=VMEM_LIMIT_BYTES),
    )(qT, k, vT, gate)


def _gla_block(lq_ref, lv_ref, lkT_ref, gT_ref, s_ref, *, lg, reverse):
    nch = lg // CHUNK
    gT = gT_ref[0]
    t_src = lax.broadcasted_iota(jnp.int32, (lg, lg), 0)
    t_dst = lax.broadcasted_iota(jnp.int32, (lg, lg), 1)
    same = (t_src // CHUNK) == (t_dst // CHUNK)
    tri = jnp.where(same & ((t_src >= t_dst) if reverse else (t_src <= t_dst)), 1.0, 0.0).astype(BF16)
    g1 = gT.astype(BF16)
    r1 = gT - g1.astype(F32)
    g2 = r1.astype(BF16)
    g3 = (r1 - g2.astype(F32)).astype(BF16)
    bT = _dot(g1, tri) + _dot(g2, tri) + _dot(g3, tri)

    lane = lax.broadcasted_iota(jnp.int32, (GLA_KEY_WIDTH, lg), 1)
    tot_cols = []
    btot = jnp.zeros_like(bT)
    for n in range(nch):
        edge = n * CHUNK if reverse else n * CHUNK + CHUNK - 1
        col = bT[:, edge:edge + 1]
        tot_cols.append(col)
        btot = jnp.where((lane // CHUNK) == n, col, btot)

    kT = lkT_ref[0]
    k_decT = (kT * jnp.exp(-bT)).astype(BF16)
    k_kvT = (kT * jnp.exp(btot - bT)).astype(BF16)
    q_dec = lq_ref[...] * (GLA_DK ** -0.5) * jnp.exp(bT.T)
    v = lv_ref[...].astype(BF16)

    intra = same.T & ((t_src < t_dst) if reverse else (t_src >= t_dst))
    order = range(nch - 1, -1, -1) if reverse else range(nch)
    outs = []
    for h in range(GLA_HEADS):
        ks = slice(h * GLA_DK, (h + 1) * GLA_DK)
        vs = slice(h * GLA_DV, (h + 1) * GLA_DV)
        qh = q_dec[:, ks].astype(BF16)
        a = _dot(qh, k_decT[ks, :])
        a = jnp.where(intra, a, 0.0).astype(BF16)
        o_h = _dot(a, v[:, vs])
        inter = [None] * nch
        state = s_ref[h]
        for n in order:
            ts = slice(n * CHUNK, (n + 1) * CHUNK)
            inter[n] = _dot(qh[ts], state.astype(BF16))
            kv = _dot(k_kvT[ks, ts], v[ts, vs])
            state = jnp.exp(tot_cols[n][ks]) * state + kv
        s_ref[h] = state
        outs.append(o_h + jnp.concatenate(inter, axis=0))
    return jnp.concatenate(outs, axis=1)


def _gla_fwd_kernel(lq_ref, lv_ref, lkT_ref, gT_ref, o_ref, s_ref, *, lg):
    @pl.when(pl.program_id(1) == 0)
    def _():
        s_ref[...] = jnp.zeros_like(s_ref)
    o_ref[...] = _gla_block(lq_ref, lv_ref, lkT_ref, gT_ref, s_ref, lg=lg, reverse=False)


def _gla_bwd_kernel(lq_ref, lv_ref, lkT_ref, gT_ref, ofwd_ref, gate_ref, nw_ref, o_ref, s_ref, *, lg):
    @pl.when(pl.program_id(1) == 0)
    def _():
        s_ref[...] = jnp.zeros_like(s_ref)
    o = ofwd_ref[...] + _gla_block(lq_ref, lv_ref, lkT_ref, gT_ref, s_ref, lg=lg, reverse=True)
    gate = gate_ref[...]
    nw = nw_ref[...]
    for h in range(GLA_HEADS):
        vs = slice(h * GLA_DV, (h + 1) * GLA_DV)
        oh = o[:, vs]
        y = oh * lax.rsqrt(jnp.mean(oh * oh, axis=-1, keepdims=True) + EPS) * nw
        o_ref[:, vs] = (y * gate[:, vs]).astype(BF16)


def _gla(lq, lv, lkT, gT, lgate, gla_nw, B, T, lg):
    nb = T // lg
    params = pltpu.CompilerParams(dimension_semantics=("arbitrary", "arbitrary"),
                                  vmem_limit_bytes=VMEM_LIMIT_BYTES)
    scratch = [pltpu.VMEM((GLA_HEADS, GLA_DK, GLA_DV), F32)]
    ftok = lambda b, i: (b * nb + i, 0)
    o_fwd = pl.pallas_call(
        functools.partial(_gla_fwd_kernel, lg=lg),
        grid=(B, nb),
        in_specs=[
            pl.BlockSpec((lg, GLA_KEY_WIDTH), ftok),
            pl.BlockSpec((lg, GLA_WIDTH), ftok),
            pl.BlockSpec((1, GLA_KEY_WIDTH, lg), lambda b, i: (b, 0, i)),
            pl.BlockSpec((1, GLA_KEY_WIDTH, lg), lambda b, i: (b, 0, i)),
        ],
        out_specs=pl.BlockSpec((lg, GLA_WIDTH), ftok),
        out_shape=jax.ShapeDtypeStruct((B * T, GLA_WIDTH), F32),
        scratch_shapes=scratch, name="gla_fwd", compiler_params=params,
    )(lq, lv, lkT, gT)
    rtok = lambda b, i: (b * nb + (nb - 1 - i), 0)
    return pl.pallas_call(
        functools.partial(_gla_bwd_kernel, lg=lg),
        grid=(B, nb),
        in_specs=[
            pl.BlockSpec((lg, GLA_KEY_WIDTH), rtok),
            pl.BlockSpec((lg, GLA_WIDTH), rtok),
            pl.BlockSpec((1, GLA_KEY_WIDTH, lg), lambda b, i: (b, 0, nb - 1 - i)),
            pl.BlockSpec((1, GLA_KEY_WIDTH, lg), lambda b, i: (b, 1, nb - 1 - i)),
            pl.BlockSpec((lg, GLA_WIDTH), rtok),
            pl.BlockSpec((lg, GLA_WIDTH), rtok),
            pl.BlockSpec((1, GLA_DV), lambda b, i: (0, 0)),
        ],
        out_specs=pl.BlockSpec((lg, GLA_WIDTH), rtok),
        out_shape=jax.ShapeDtypeStruct((B * T, GLA_WIDTH), BF16),
        scratch_shapes=scratch, name="gla_bwd", compiler_params=params,
    )(lq, lv, lkT, gT, o_fwd, lgate, gla_nw)


def _out_kernel(x_ref, ma_ref, mg_ref, p_ref, wo_ref, plew_ref, wg_ref, wp_ref, fw_ref, y_ref, *, final):
    h = x_ref[...] + _dot(ma_ref[...], wo_ref[0:ATTN_WIDTH, :]) + _dot(mg_ref[...], wo_ref[ATTN_WIDTH:, :])
    hn = h * lax.rsqrt(jnp.mean(h * h, axis=-1, keepdims=True) + EPS) * plew_ref[...]
    gate = jax.nn.sigmoid(_dot(hn.astype(BF16), wg_ref[...]))
    h = h + gate * _dot(p_ref[...].astype(BF16), wp_ref[...])
    if final:
        h = h * lax.rsqrt(jnp.mean(h * h, axis=-1, keepdims=True) + EPS) * fw_ref[...]
    y_ref[...] = h


def _outproj(x2d, ma, mg, p2d, wo, plew, wg, wp, fw, tm, final):
    n = x2d.shape[0]
    tok = lambda i: (i, 0)
    const = lambda i: (0, 0)
    return pl.pallas_call(
        functools.partial(_out_kernel, final=final),
        grid=(n // tm,),
        in_specs=[
            pl.BlockSpec((tm, D_MODEL), tok),
            pl.BlockSpec((tm, ATTN_WIDTH), tok),
            pl.BlockSpec((tm, GLA_WIDTH), tok),
            pl.BlockSpec((tm, PLE_DIM), tok),
            pl.BlockSpec(wo.shape, const),
            pl.BlockSpec(plew.shape, const),
            pl.BlockSpec(wg.shape, const),
            pl.BlockSpec(wp.shape, const),
            pl.BlockSpec(fw.shape, const),
        ],
        out_specs=pl.BlockSpec((tm, D_MODEL), tok),
        out_shape=jax.ShapeDtypeStruct((n, D_MODEL), F32),
        name="outproj",
        compiler_params=pltpu.CompilerParams(
            dimension_semantics=("arbitrary",), vmem_limit_bytes=VMEM_LIMIT_BYTES),
    )(x2d, ma, mg, p2d, wo, plew, wg, wp, fw)


def _rope_tables(T):
    t = jnp.arange(T, dtype=jnp.int32)
    row = (t // GRID_W).astype(F32)
    col = (t % GRID_W).astype(F32)
    half = HEAD_DIM // 2
    inv_freq = ROPE_THETA ** (-jnp.arange(0, half, 2, dtype=F32) / half)
    ang_r = inv_freq[:, None] * row[None, :]
    ang_c = inv_freq[:, None] * col[None, :]
    return jnp.stack([jnp.cos(ang_r), jnp.sin(ang_r), jnp.cos(ang_c), jnp.sin(ang_c)])


def _prep_layer_weights(mix_norm, w_in, q_norm, k_norm, w_up_f, b_f, w_up_b, b_b, gla_norm, w_out,
                        ple_norm, w_ple_gate, w_ple_proj):
    c = [0, 512, 640, 768, 1280, 1536, 1792, 2304, 2816, 2832, 2848]
    seg = lambda j: w_in[:, c[j]:c[j + 1]]
    a_q, a_k, a_v, a_gate, l_q, l_k, l_v, l_gate, lr_f, lr_b = (seg(j) for j in range(10))
    wrow = jnp.concatenate([a_gate, l_q, l_v, l_gate], axis=1).astype(BF16)
    wt = jnp.concatenate([a_q, a_k, a_v, l_k, lr_f, lr_b], axis=1).T.astype(BF16)
    qs = q_norm * (HEAD_DIM ** -0.5 * LOG2E)
    nw = jnp.concatenate([jnp.tile(qs[None], (N_Q_HEADS, 1)), jnp.tile(k_norm[None], (N_KV_HEADS, 1))])[..., None]
    zero = jnp.zeros((GLA_KEY_WIDTH, GATE_RANK), F32)
    wup = jnp.concatenate([jnp.concatenate([w_up_f.T, zero], axis=1),
                           jnp.concatenate([zero, w_up_b.T], axis=1)], axis=0).astype(BF16)
    bup = jnp.concatenate([b_f, b_b])[:, None]
    return dict(mixw=mix_norm[None], wrow=wrow, wt=wt, nw=nw, wup=wup, bup=bup, gla_nw=gla_norm[None],
                wo=w_out.astype(BF16), plew=ple_norm[None], wg=w_ple_gate.astype(BF16),
                wp=w_ple_proj.astype(BF16))


def _layer(h2d, p2d, B, T, w, fw, final, tiles):
    tm, tq, lg, tmo = tiles
    rope = _rope_tables(T)
    qT, k, vT, ag, lq, lv, lgate, lkT, gT = _inproj(
        h2d, B, T, tm, w["mixw"], w["wrow"], w["wt"], w["nw"], rope, w["wup"], w["bup"])
    ma = _attention(qT, k, vT, ag, B, T, tq, tm)
    mg = _gla(lq, lv, lkT, gT, lgate, w["gla_nw"], B, T, lg)
    return _outproj(h2d, ma, mg, p2d, w["wo"], w["plew"], w["wg"], w["wp"], fw, tmo, final)


def _pick_tiles(T):
    return (min(512, T), min(256, T), min(256, T), min(512, T))


def kernel(x_prompt, x_sample, p_prompt, p_sample, mix_norm, w_in, q_norm, k_norm, w_gate_up_fwd,
           b_gate_fwd, w_gate_up_bwd, b_gate_bwd, gla_norm, w_out, ple_norm, w_ple_gate, w_ple_proj,
           final_norm):
    depth = w_in.shape[0]
    fw = final_norm[None]
    outs = []
    for x, p in ((x_prompt, p_prompt), (x_sample, p_sample)):
        B, T, _ = x.shape
        h = x.reshape(B * T, D_MODEL)
        for i in range(depth):
            w = _prep_layer_weights(mix_norm[i], w_in[i], q_norm[i], k_norm[i], w_gate_up_fwd[i],
                                    b_gate_fwd[i], w_gate_up_bwd[i], b_gate_bwd[i], gla_norm[i], w_out[i],
                                    ple_norm[i], w_ple_gate[i], w_ple_proj[i])
            h = _layer(h, p[i].reshape(B * T, PLE_DIM), B, T, w, fw, i == depth - 1, _pick_tiles(T))
        outs.append(h.reshape(B, T, D_MODEL))
    return tuple(outs)
```

```python
import functools
import math

import jax
import jax.numpy as jnp
from jax import lax
from jax.experimental import pallas as pl
from jax.experimental.pallas import tpu as pltpu

F32 = jnp.float32
BF16 = jnp.bfloat16

D_MODEL = 1024
GRID_W = 64
N_Q_HEADS = 8
N_KV_HEADS = 2
HEAD_DIM = 64
Q_PER_KV = N_Q_HEADS // N_KV_HEADS
ATTN_WIDTH = N_Q_HEADS * HEAD_DIM
KV_WIDTH = N_KV_HEADS * HEAD_DIM
ROPE_THETA = 10000.0
GLA_HEADS = 4
GLA_DK = 64
GLA_DV = 128
GLA_KEY_WIDTH = GLA_HEADS * GLA_DK
GLA_WIDTH = GLA_HEADS * GLA_DV
GATE_RANK = 16
GATE_NORMALIZER = 16.0
CHUNK = 64
PLE_DIM = 256
EPS = 1e-6
LOG2E = math.log2(math.e)

VMEM_LIMIT_BYTES = 56 * 1024 * 1024
BF16_SUBLANES = 16
V_ROWS = HEAD_DIM + BF16_SUBLANES
NEG_BIG = -1e30
Q_COLS = 512
SHIFT_HEADROOM = 100.0
MIN_DENOMINATOR = 2.0 ** -60

NT_DIMS = (((1,), (1,)), ((), ()))


def _dot(a, b):
    return jnp.dot(a, b, preferred_element_type=F32)


def _silu(x):
    return x * jax.nn.sigmoid(x)


def _inproj_kernel(x_ref, mixw_ref, wrow_ref, wt_ref, nw_ref, rope_ref, wup_ref, bup_ref,
                   qT_ref, k_ref, vT_ref, ag_ref, lq_ref, lv_ref, lg_ref, lkT_ref, gT_ref):
    x = x_ref[...]
    xn = x * lax.rsqrt(jnp.mean(x * x, axis=-1, keepdims=True) + EPS) * mixw_ref[...]
    xn = xn.astype(BF16)

    zr = _dot(xn, wrow_ref[...])
    ag_ref[...] = _silu(zr[:, 0:ATTN_WIDTH])
    lq_ref[...] = zr[:, 512:768]
    lv_ref[...] = zr[:, 768:1280]
    lg_ref[...] = _silu(zr[:, 1280:1792])

    zt = lax.dot_general(wt_ref[...], xn, NT_DIMS, preferred_element_type=F32)
    tm = zt.shape[1]
    cr, sr, cc, sc = rope_ref[0], rope_ref[1], rope_ref[2], rope_ref[3]

    def norm_rope(zh, w):
        y = zh * lax.rsqrt(jnp.mean(zh * zh, axis=0, keepdims=True) + EPS) * w
        r1, r2, c1, c2 = y[0:16], y[16:32], y[32:48], y[48:64]
        return jnp.concatenate([r1 * cr - r2 * sr, r2 * cr + r1 * sr,
                                c1 * cc - c2 * sc, c2 * cc + c1 * sc], axis=0)

    for h in range(N_Q_HEADS):
        qh = norm_rope(zt[h * HEAD_DIM:(h + 1) * HEAD_DIM], nw_ref[h])
        qT_ref[0, h * HEAD_DIM:(h + 1) * HEAD_DIM, :] = qh.astype(BF16)
    kT = jnp.concatenate(
        [norm_rope(zt[ATTN_WIDTH + h * HEAD_DIM:ATTN_WIDTH + (h + 1) * HEAD_DIM], nw_ref[N_Q_HEADS + h])
         for h in range(N_KV_HEADS)], axis=0)
    k_ref[...] = kT.T.astype(BF16)

    ones = jnp.ones((BF16_SUBLANES, tm), F32)
    v0 = ATTN_WIDTH + KV_WIDTH
    vT = jnp.concatenate([zt[v0:v0 + HEAD_DIM], ones, zt[v0 + HEAD_DIM:v0 + 2 * HEAD_DIM], ones], axis=0)
    vT_ref[0, 0] = vT.astype(BF16)

    l0 = v0 + KV_WIDTH
    lkT_ref[0] = zt[l0:l0 + GLA_KEY_WIDTH]
    lrT = zt[l0 + GLA_KEY_WIDTH:l0 + GLA_KEY_WIDTH + 2 * GATE_RANK].astype(BF16)
    gpre = _dot(wup_ref[...], lrT) + bup_ref[...]
    gT_ref[0] = jax.nn.log_sigmoid(gpre) * (1.0 / GATE_NORMALIZER)


def _inproj(x2d, B, T, tm, mixw, wrow, wt, nw, rope, wup, bup):
    nt = T // tm
    grid = (B, nt)
    tok = lambda b, i: (b * nt + i, 0)
    fm = lambda b, i: (b, 0, i)
    const2 = lambda b, i: (0, 0)
    const3 = lambda b, i: (0, 0, 0)
    out_shape = (
        jax.ShapeDtypeStruct((B, ATTN_WIDTH, T), BF16),
        jax.ShapeDtypeStruct((B * T, KV_WIDTH), BF16),
        jax.ShapeDtypeStruct((B, nt, N_KV_HEADS * V_ROWS, tm), BF16),
        jax.ShapeDtypeStruct((B * T, ATTN_WIDTH), F32),
        jax.ShapeDtypeStruct((B * T, GLA_KEY_WIDTH), F32),
        jax.ShapeDtypeStruct((B * T, GLA_WIDTH), F32),
        jax.ShapeDtypeStruct((B * T, GLA_WIDTH), F32),
        jax.ShapeDtypeStruct((B, GLA_KEY_WIDTH, T), F32),
        jax.ShapeDtypeStruct((B, 2 * GLA_KEY_WIDTH, T), F32),
    )
    out_specs = (
        pl.BlockSpec((1, ATTN_WIDTH, tm), fm),
        pl.BlockSpec((tm, KV_WIDTH), tok),
        pl.BlockSpec((1, 1, N_KV_HEADS * V_ROWS, tm), lambda b, i: (b, i, 0, 0)),
        pl.BlockSpec((tm, ATTN_WIDTH), tok),
        pl.BlockSpec((tm, GLA_KEY_WIDTH), tok),
        pl.BlockSpec((tm, GLA_WIDTH), tok),
        pl.BlockSpec((tm, GLA_WIDTH), tok),
        pl.BlockSpec((1, GLA_KEY_WIDTH, tm), fm),
        pl.BlockSpec((1, 2 * GLA_KEY_WIDTH, tm), fm),
    )
    in_specs = [
        pl.BlockSpec((tm, D_MODEL), tok),
        pl.BlockSpec(mixw.shape, const2),
        pl.BlockSpec(wrow.shape, const2),
        pl.BlockSpec(wt.shape, const2),
        pl.BlockSpec(nw.shape, const3),
        pl.BlockSpec((4, HEAD_DIM // 4, tm), lambda b, i: (0, 0, i)),
        pl.BlockSpec(wup.shape, const2),
        pl.BlockSpec(bup.shape, const2),
    ]
    return pl.pallas_call(
        _inproj_kernel, grid=grid, in_specs=in_specs, out_specs=out_specs, out_shape=out_shape,
        name="inproj",
        compiler_params=pltpu.CompilerParams(
            dimension_semantics=("arbitrary", "arbitrary"), vmem_limit_bytes=VMEM_LIMIT_BYTES),
    )(x2d, mixw, wrow, wt, nw, rope, wup, bup)


def _attn_kernel(qT_ref, k_ref, vT_ref, gate_ref, kb_ref, o_ref, w_ref, m_ref, acc_ref, sa_ref, sb_ref,
                 *, tq, kc, nchunks):
    assert nchunks % 2 == 0
    g = pl.program_id(1)
    nq = Q_PER_KV * tq
    qcat = jnp.concatenate([qT_ref[0, h * HEAD_DIM:(h + 1) * HEAD_DIM, :] for h in range(Q_PER_KV)], axis=1)
    zeros = jnp.zeros_like(qcat)
    w_ref[...] = jnp.where(g == 0, jnp.concatenate([qcat, zeros], axis=0), jnp.concatenate([zeros, qcat], axis=0))

    qf = qcat.astype(F32)
    m_ref[...] = jnp.sqrt(jnp.sum(qf * qf, axis=0, keepdims=True)) * kb_ref[...] - SHIFT_HEADROOM
    acc_ref[...] = jnp.zeros((V_ROWS, nq), F32)

    def key_chunk(c):
        start = c * kc if isinstance(c, int) else pl.multiple_of(c * kc, kc)
        return k_ref[pl.ds(start, kc), :]

    def step(c, s_cur, s_nxt):
        kch = None if s_nxt is None else key_chunk(c + 1)
        vch = None if s_cur is None else vT_ref[0, c]
        for j in range(nq // Q_COLS):
            cols = slice(j * Q_COLS, (j + 1) * Q_COLS)
            if s_nxt is not None:
                s_nxt[:, cols] = _dot(kch, w_ref[:, cols])
            if s_cur is not None:
                p = jnp.exp2(s_cur[:, cols] - m_ref[:, cols]).astype(BF16)
                acc_ref[:, cols] += _dot(vch, p)

    def two_steps(i, carry):
        c = 2 * i
        step(c, sa_ref, sb_ref)
        step(c + 1, sb_ref, sa_ref)
        return carry

    step(-1, None, sa_ref)
    lax.fori_loop(0, nchunks // 2 - 1, two_steps, 0)
    step(nchunks - 2, sa_ref, sb_ref)
    step(nchunks - 1, sb_ref, None)

    @pl.when(jnp.logical_not(jnp.min(acc_ref[HEAD_DIM:HEAD_DIM + 1, :]) >= MIN_DENOMINATOR))
    def _():
        m_ref[...] = jnp.full((1, nq), NEG_BIG, F32)
        acc_ref[...] = jnp.zeros((V_ROWS, nq), F32)

        def running_max_body(c, carry):
            s = _dot(key_chunk(c), w_ref[...])
            m_old = m_ref[...]
            m_new = jnp.maximum(m_old, jnp.max(s, axis=0, keepdims=True))
            p = jnp.exp2(s - m_new).astype(BF16)
            acc_ref[...] = acc_ref[...] * jnp.exp2(m_old - m_new) + _dot(vT_ref[0, c], p)
            m_ref[...] = m_new
            return carry

        lax.fori_loop(0, nchunks, running_max_body, 0)

    acc = acc_ref[...]
    o = acc[0:HEAD_DIM] / acc[HEAD_DIM:HEAD_DIM + 1]
    oT = jnp.concatenate([o[:, h * tq:(h + 1) * tq] for h in range(Q_PER_KV)], axis=0)
    o_ref[...] = (oT.T * gate_ref[...]).astype(BF16)


def _attention(qT, k, vT, gate, kbound, B, T, tq, kc):
    nq_tiles = T // tq
    nchunks = T // kc
    gw = Q_PER_KV * HEAD_DIM
    nq = Q_PER_KV * tq
    kern = functools.partial(_attn_kernel, tq=tq, kc=kc, nchunks=nchunks)
    return pl.pallas_call(
        kern,
        grid=(B, N_KV_HEADS, nq_tiles),
        in_specs=[
            pl.BlockSpec((1, gw, tq), lambda b, g, i: (b, g, i)),
            pl.BlockSpec((T, KV_WIDTH), lambda b, g, i: (b, 0)),
            pl.BlockSpec((1, nchunks, V_ROWS, kc), lambda b, g, i: (b, 0, g, 0)),
            pl.BlockSpec((tq, gw), lambda b, g, i: (b * nq_tiles + i, g)),
            pl.BlockSpec((1, 1), lambda b, g, i: (0, 0)),
        ],
        out_specs=pl.BlockSpec((tq, gw), lambda b, g, i: (b * nq_tiles + i, g)),
        out_shape=jax.ShapeDtypeStruct((B * T, ATTN_WIDTH), BF16),
        scratch_shapes=[pltpu.VMEM((KV_WIDTH, nq), BF16), pltpu.VMEM((1, nq), F32),
                        pltpu.VMEM((V_ROWS, nq), F32), pltpu.VMEM((kc, nq), F32), pltpu.VMEM((kc, nq), F32)],
        name="attention",
        compiler_params=pltpu.CompilerParams(
            dimension_semantics=("arbitrary", "arbitrary", "arbitrary"), vmem_limit_bytes=VMEM_LIMIT_BYTES),
    )(qT, k, vT, gate, kbound)


def _gla_block(lq_ref, lv_ref, lkT_ref, gT_ref, s_ref, *, lg, reverse):
    nch = lg // CHUNK
    gT = gT_ref[0]
    t_src = lax.broadcasted_iota(jnp.int32, (lg, lg), 0)
    t_dst = lax.broadcasted_iota(jnp.int32, (lg, lg), 1)
    same = (t_src // CHUNK) == (t_dst // CHUNK)
    tri = jnp.where(same & ((t_src >= t_dst) if reverse else (t_src <= t_dst)), 1.0, 0.0).astype(BF16)
    g1 = gT.astype(BF16)
    r1 = gT - g1.astype(F32)
    g2 = r1.astype(BF16)
    g3 = (r1 - g2.astype(F32)).astype(BF16)
    bT = _dot(g1, tri) + _dot(g2, tri) + _dot(g3, tri)

    lane = lax.broadcasted_iota(jnp.int32, (GLA_KEY_WIDTH, lg), 1)
    tot_cols = []
    btot = jnp.zeros_like(bT)
    for n in range(nch):
        edge = n * CHUNK if reverse else n * CHUNK + CHUNK - 1
        col = bT[:, edge:edge + 1]
        tot_cols.append(col)
        btot = jnp.where((lane // CHUNK) == n, col, btot)

    kT = lkT_ref[0]
    k_decT = (kT * jnp.exp(-bT)).astype(BF16)
    k_kvT = (kT * jnp.exp(btot - bT)).astype(BF16)
    q_dec = lq_ref[...] * (GLA_DK ** -0.5) * jnp.exp(bT.T)
    v = lv_ref[...].astype(BF16)

    intra = same.T & ((t_src < t_dst) if reverse else (t_src >= t_dst))
    order = range(nch - 1, -1, -1) if reverse else range(nch)
    outs = []
    for h in range(GLA_HEADS):
        ks = slice(h * GLA_DK, (h + 1) * GLA_DK)
        vs = slice(h * GLA_DV, (h + 1) * GLA_DV)
        qh = q_dec[:, ks].astype(BF16)
        a = _dot(qh, k_decT[ks, :])
        a = jnp.where(intra, a, 0.0).astype(BF16)
        o_h = _dot(a, v[:, vs])
        inter = [None] * nch
        state = s_ref[h]
        for n in order:
            ts = slice(n * CHUNK, (n + 1) * CHUNK)
            inter[n] = _dot(qh[ts], state.astype(BF16))
            kv = _dot(k_kvT[ks, ts], v[ts, vs])
            state = jnp.exp(tot_cols[n][ks]) * state + kv
        s_ref[h] = state
        outs.append(o_h + jnp.concatenate(inter, axis=0))
    return jnp.concatenate(outs, axis=1)


def _gla_fwd_kernel(lq_ref, lv_ref, lkT_ref, gT_ref, o_ref, s_ref, *, lg):
    @pl.when(pl.program_id(1) == 0)
    def _():
        s_ref[...] = jnp.zeros_like(s_ref)
    o_ref[...] = _gla_block(lq_ref, lv_ref, lkT_ref, gT_ref, s_ref, lg=lg, reverse=False)


def _gla_bwd_kernel(lq_ref, lv_ref, lkT_ref, gT_ref, ofwd_ref, gate_ref, nw_ref, o_ref, s_ref, *, lg):
    @pl.when(pl.program_id(1) == 0)
    def _():
        s_ref[...] = jnp.zeros_like(s_ref)
    o = ofwd_ref[...] + _gla_block(lq_ref, lv_ref, lkT_ref, gT_ref, s_ref, lg=lg, reverse=True)
    gate = gate_ref[...]
    nw = nw_ref[...]
    for h in range(GLA_HEADS):
        vs = slice(h * GLA_DV, (h + 1) * GLA_DV)
        oh = o[:, vs]
        y = oh * lax.rsqrt(jnp.mean(oh * oh, axis=-1, keepdims=True) + EPS) * nw
        o_ref[:, vs] = (y * gate[:, vs]).astype(BF16)


def _gla(lq, lv, lkT, gT, lgate, gla_nw, B, T, lg):
    nb = T // lg
    params = pltpu.CompilerParams(dimension_semantics=("arbitrary", "arbitrary"),
                                  vmem_limit_bytes=VMEM_LIMIT_BYTES)
    scratch = [pltpu.VMEM((GLA_HEADS, GLA_DK, GLA_DV), F32)]
    ftok = lambda b, i: (b * nb + i, 0)
    o_fwd = pl.pallas_call(
        functools.partial(_gla_fwd_kernel, lg=lg),
        grid=(B, nb),
        in_specs=[
            pl.BlockSpec((lg, GLA_KEY_WIDTH), ftok),
            pl.BlockSpec((lg, GLA_WIDTH), ftok),
            pl.BlockSpec((1, GLA_KEY_WIDTH, lg), lambda b, i: (b, 0, i)),
            pl.BlockSpec((1, GLA_KEY_WIDTH, lg), lambda b, i: (b, 0, i)),
        ],
        out_specs=pl.BlockSpec((lg, GLA_WIDTH), ftok),
        out_shape=jax.ShapeDtypeStruct((B * T, GLA_WIDTH), F32),
        scratch_shapes=scratch, name="gla_fwd", compiler_params=params,
    )(lq, lv, lkT, gT)
    rtok = lambda b, i: (b * nb + (nb - 1 - i), 0)
    return pl.pallas_call(
        functools.partial(_gla_bwd_kernel, lg=lg),
        grid=(B, nb),
        in_specs=[
            pl.BlockSpec((lg, GLA_KEY_WIDTH), rtok),
            pl.BlockSpec((lg, GLA_WIDTH), rtok),
            pl.BlockSpec((1, GLA_KEY_WIDTH, lg), lambda b, i: (b, 0, nb - 1 - i)),
            pl.BlockSpec((1, GLA_KEY_WIDTH, lg), lambda b, i: (b, 1, nb - 1 - i)),
            pl.BlockSpec((lg, GLA_WIDTH), rtok),
            pl.BlockSpec((lg, GLA_WIDTH), rtok),
            pl.BlockSpec((1, GLA_DV), lambda b, i: (0, 0)),
        ],
        out_specs=pl.BlockSpec((lg, GLA_WIDTH), rtok),
        out_shape=jax.ShapeDtypeStruct((B * T, GLA_WIDTH), BF16),
        scratch_shapes=scratch, name="gla_bwd", compiler_params=params,
    )(lq, lv, lkT, gT, o_fwd, lgate, gla_nw)


def _out_kernel(x_ref, ma_ref, mg_ref, p_ref, wo_ref, plew_ref, wg_ref, wp_ref, fw_ref, y_ref, *, final):
    h = x_ref[...] + _dot(ma_ref[...], wo_ref[0:ATTN_WIDTH, :]) + _dot(mg_ref[...], wo_ref[ATTN_WIDTH:, :])
    hn = h * lax.rsqrt(jnp.mean(h * h, axis=-1, keepdims=True) + EPS) * plew_ref[...]
    gate = jax.nn.sigmoid(_dot(hn.astype(BF16), wg_ref[...]))
    h = h + gate * _dot(p_ref[...].astype(BF16), wp_ref[...])
    if final:
        h = h * lax.rsqrt(jnp.mean(h * h, axis=-1, keepdims=True) + EPS) * fw_ref[...]
    y_ref[...] = h


def _outproj(x2d, ma, mg, p2d, wo, plew, wg, wp, fw, tm, final):
    n = x2d.shape[0]
    tok = lambda i: (i, 0)
    const = lambda i: (0, 0)
    return pl.pallas_call(
        functools.partial(_out_kernel, final=final),
        grid=(n // tm,),
        in_specs=[
            pl.BlockSpec((tm, D_MODEL), tok),
            pl.BlockSpec((tm, ATTN_WIDTH), tok),
            pl.BlockSpec((tm, GLA_WIDTH), tok),
            pl.BlockSpec((tm, PLE_DIM), tok),
            pl.BlockSpec(wo.shape, const),
            pl.BlockSpec(plew.shape, const),
            pl.BlockSpec(wg.shape, const),
            pl.BlockSpec(wp.shape, const),
            pl.BlockSpec(fw.shape, const),
        ],
        out_specs=pl.BlockSpec((tm, D_MODEL), tok),
        out_shape=jax.ShapeDtypeStruct((n, D_MODEL), F32),
        name="outproj",
        compiler_params=pltpu.CompilerParams(
            dimension_semantics=("arbitrary",), vmem_limit_bytes=VMEM_LIMIT_BYTES),
    )(x2d, ma, mg, p2d, wo, plew, wg, wp, fw)


def _rope_tables(T):
    t = jnp.arange(T, dtype=jnp.int32)
    row = (t // GRID_W).astype(F32)
    col = (t % GRID_W).astype(F32)
    half = HEAD_DIM // 2
    inv_freq = ROPE_THETA ** (-jnp.arange(0, half, 2, dtype=F32) / half)
    ang_r = inv_freq[:, None] * row[None, :]
    ang_c = inv_freq[:, None] * col[None, :]
    return jnp.stack([jnp.cos(ang_r), jnp.sin(ang_r), jnp.cos(ang_c), jnp.sin(ang_c)])


def _prep_layer_weights(mix_norm, w_in, q_norm, k_norm, w_up_f, b_f, w_up_b, b_b, gla_norm, w_out,
                        ple_norm, w_ple_gate, w_ple_proj):
    c = [0, 512, 640, 768, 1280, 1536, 1792, 2304, 2816, 2832, 2848]
    seg = lambda j: w_in[:, c[j]:c[j + 1]]
    a_q, a_k, a_v, a_gate, l_q, l_k, l_v, l_gate, lr_f, lr_b = (seg(j) for j in range(10))
    wrow = jnp.concatenate([a_gate, l_q, l_v, l_gate], axis=1).astype(BF16)
    wt = jnp.concatenate([a_q, a_k, a_v, l_k, lr_f, lr_b], axis=1).T.astype(BF16)
    qs = q_norm * (HEAD_DIM ** -0.5 * LOG2E)
    nw = jnp.concatenate([jnp.tile(qs[None], (N_Q_HEADS, 1)), jnp.tile(k_norm[None], (N_KV_HEADS, 1))])[..., None]
    zero = jnp.zeros((GLA_KEY_WIDTH, GATE_RANK), F32)
    wup = jnp.concatenate([jnp.concatenate([w_up_f.T, zero], axis=1),
                           jnp.concatenate([zero, w_up_b.T], axis=1)], axis=0).astype(BF16)
    bup = jnp.concatenate([b_f, b_b])[:, None]
    kbound = (jnp.max(jnp.abs(k_norm)) * (HEAD_DIM ** 0.5 * (1.0 + 2.0 ** -7))).reshape(1, 1)
    return dict(mixw=mix_norm[None], wrow=wrow, wt=wt, nw=nw, wup=wup, bup=bup, kbound=kbound,
                gla_nw=gla_norm[None],
                wo=w_out.astype(BF16), plew=ple_norm[None], wg=w_ple_gate.astype(BF16),
                wp=w_ple_proj.astype(BF16))


def _layer(h2d, p2d, B, T, w, fw, final, tiles):
    tm, tq, lg, tmo = tiles
    rope = _rope_tables(T)
    qT, k, vT, ag, lq, lv, lgate, lkT, gT = _inproj(
        h2d, B, T, tm, w["mixw"], w["wrow"], w["wt"], w["nw"], rope, w["wup"], w["bup"])
    ma = _attention(qT, k, vT, ag, w["kbound"], B, T, tq, tm)
    mg = _gla(lq, lv, lkT, gT, lgate, w["gla_nw"], B, T, lg)
    return _outproj(h2d, ma, mg, p2d, w["wo"], w["plew"], w["wg"], w["wp"], fw, tmo, final)


def _pick_tiles(T):
    return (min(512, T), min(256, T), min(256, T), min(512, T))


def kernel(x_prompt, x_sample, p_prompt, p_sample, mix_norm, w_in, q_norm, k_norm, w_gate_up_fwd,
           b_gate_fwd, w_gate_up_bwd, b_gate_bwd, gla_norm, w_out, ple_norm, w_ple_gate, w_ple_proj,
           final_norm):
    depth = w_in.shape[0]
    fw = final_norm[None]
    outs = []
    for x, p in ((x_prompt, p_prompt), (x_sample, p_sample)):
        B, T, _ = x.shape
        h = x.reshape(B * T, D_MODEL)
        for i in range(depth):
            w = _prep_layer_weights(mix_norm[i], w_in[i], q_norm[i], k_norm[i], w_gate_up_fwd[i],
                                    b_gate_fwd[i], w_gate_up_bwd[i], b_gate_bwd[i], gla_norm[i], w_out[i],
                                    ple_norm[i], w_ple_gate[i], w_ple_proj[i])
            h = _layer(h, p[i].reshape(B * T, PLE_DIM), B, T, w, fw, i == depth - 1, _pick_tiles(T))
        outs.append(h.reshape(B, T, D_MODEL))
    return tuple(outs)
```

```python
import functools
import math

import jax
import jax.numpy as jnp
from jax import lax
from jax.experimental import pallas as pl
from jax.experimental.pallas import tpu as pltpu

F32 = jnp.float32
BF16 = jnp.bfloat16

D_MODEL = 1024
GRID_W = 64
N_Q_HEADS = 8
N_KV_HEADS = 2
HEAD_DIM = 64
Q_PER_KV = N_Q_HEADS // N_KV_HEADS
ATTN_WIDTH = N_Q_HEADS * HEAD_DIM
KV_WIDTH = N_KV_HEADS * HEAD_DIM
ROPE_THETA = 10000.0
GLA_HEADS = 4
GLA_DK = 64
GLA_DV = 128
GLA_KEY_WIDTH = GLA_HEADS * GLA_DK
GLA_WIDTH = GLA_HEADS * GLA_DV
GATE_RANK = 16
GATE_NORMALIZER = 16.0
CHUNK = 64
PLE_DIM = 256
EPS = 1e-6
LOG2E = math.log2(math.e)

VMEM_LIMIT_BYTES = 56 * 1024 * 1024
BF16_SUBLANES = 16
V_ROWS = HEAD_DIM + BF16_SUBLANES
NEG_BIG = -1e30
Q_COLS = 512
KEY_ROWS = 256
SHIFT_HEADROOM = 100.0
MIN_DENOMINATOR = 2.0 ** -60

NT_DIMS = (((1,), (1,)), ((), ()))


def _dot(a, b):
    return jnp.dot(a, b, preferred_element_type=F32)


def _silu(x):
    return x * jax.nn.sigmoid(x)


def _inproj_kernel(x_ref, mixw_ref, wrow_ref, wt_ref, nw_ref, rope_ref, wup_ref, bup_ref,
                   qT_ref, k_ref, vT_ref, ag_ref, lq_ref, lv_ref, lg_ref, lkT_ref, gT_ref):
    x = x_ref[...]
    xn = x * lax.rsqrt(jnp.mean(x * x, axis=-1, keepdims=True) + EPS) * mixw_ref[...]
    xn = xn.astype(BF16)

    zr = _dot(xn, wrow_ref[...])
    ag_ref[...] = _silu(zr[:, 0:ATTN_WIDTH])
    lq_ref[...] = zr[:, 512:768]
    lv_ref[...] = zr[:, 768:1280]
    lg_ref[...] = _silu(zr[:, 1280:1792])

    zt = lax.dot_general(wt_ref[...], xn, NT_DIMS, preferred_element_type=F32)
    tm = zt.shape[1]
    cr, sr, cc, sc = rope_ref[0], rope_ref[1], rope_ref[2], rope_ref[3]

    def norm_rope(zh, w):
        y = zh * lax.rsqrt(jnp.mean(zh * zh, axis=0, keepdims=True) + EPS) * w
        r1, r2, c1, c2 = y[0:16], y[16:32], y[32:48], y[48:64]
        return jnp.concatenate([r1 * cr - r2 * sr, r2 * cr + r1 * sr,
                                c1 * cc - c2 * sc, c2 * cc + c1 * sc], axis=0)

    for h in range(N_Q_HEADS):
        qh = norm_rope(zt[h * HEAD_DIM:(h + 1) * HEAD_DIM], nw_ref[h])
        qT_ref[0, h * HEAD_DIM:(h + 1) * HEAD_DIM, :] = qh.astype(BF16)
    kT = jnp.concatenate(
        [norm_rope(zt[ATTN_WIDTH + h * HEAD_DIM:ATTN_WIDTH + (h + 1) * HEAD_DIM], nw_ref[N_Q_HEADS + h])
         for h in range(N_KV_HEADS)], axis=0)
    k_ref[...] = kT.T.astype(BF16)

    ones = jnp.ones((BF16_SUBLANES, tm), F32)
    v0 = ATTN_WIDTH + KV_WIDTH
    vT = jnp.concatenate([zt[v0:v0 + HEAD_DIM], ones, zt[v0 + HEAD_DIM:v0 + 2 * HEAD_DIM], ones], axis=0)
    vT_ref[0, 0] = vT.astype(BF16)

    l0 = v0 + KV_WIDTH
    lkT_ref[0] = zt[l0:l0 + GLA_KEY_WIDTH]
    lrT = zt[l0 + GLA_KEY_WIDTH:l0 + GLA_KEY_WIDTH + 2 * GATE_RANK].astype(BF16)
    gpre = _dot(wup_ref[...], lrT) + bup_ref[...]
    gT_ref[0] = jax.nn.log_sigmoid(gpre) * (1.0 / GATE_NORMALIZER)


def _inproj(x2d, B, T, tm, kc, mixw, wrow, wt, nw, rope, wup, bup):
    nt = T // tm
    per_kc = kc // tm
    grid = (B, nt)
    tok = lambda b, i: (b * nt + i, 0)
    fm = lambda b, i: (b, 0, i)
    const2 = lambda b, i: (0, 0)
    const3 = lambda b, i: (0, 0, 0)
    out_shape = (
        jax.ShapeDtypeStruct((B, ATTN_WIDTH, T), BF16),
        jax.ShapeDtypeStruct((B * T, KV_WIDTH), BF16),
        jax.ShapeDtypeStruct((B, T // kc, N_KV_HEADS * V_ROWS, kc), BF16),
        jax.ShapeDtypeStruct((B * T, ATTN_WIDTH), F32),
        jax.ShapeDtypeStruct((B * T, GLA_KEY_WIDTH), F32),
        jax.ShapeDtypeStruct((B * T, GLA_WIDTH), F32),
        jax.ShapeDtypeStruct((B * T, GLA_WIDTH), F32),
        jax.ShapeDtypeStruct((B, GLA_KEY_WIDTH, T), F32),
        jax.ShapeDtypeStruct((B, 2 * GLA_KEY_WIDTH, T), F32),
    )
    out_specs = (
        pl.BlockSpec((1, ATTN_WIDTH, tm), fm),
        pl.BlockSpec((tm, KV_WIDTH), tok),
        pl.BlockSpec((1, 1, N_KV_HEADS * V_ROWS, tm), lambda b, i: (b, i // per_kc, 0, i % per_kc)),
        pl.BlockSpec((tm, ATTN_WIDTH), tok),
        pl.BlockSpec((tm, GLA_KEY_WIDTH), tok),
        pl.BlockSpec((tm, GLA_WIDTH), tok),
        pl.BlockSpec((tm, GLA_WIDTH), tok),
        pl.BlockSpec((1, GLA_KEY_WIDTH, tm), fm),
        pl.BlockSpec((1, 2 * GLA_KEY_WIDTH, tm), fm),
    )
    in_specs = [
        pl.BlockSpec((tm, D_MODEL), tok),
        pl.BlockSpec(mixw.shape, const2),
        pl.BlockSpec(wrow.shape, const2),
        pl.BlockSpec(wt.shape, const2),
        pl.BlockSpec(nw.shape, const3),
        pl.BlockSpec((4, HEAD_DIM // 4, tm), lambda b, i: (0, 0, i)),
        pl.BlockSpec(wup.shape, const2),
        pl.BlockSpec(bup.shape, const2),
    ]
    return pl.pallas_call(
        _inproj_kernel, grid=grid, in_specs=in_specs, out_specs=out_specs, out_shape=out_shape,
        name="inproj",
        compiler_params=pltpu.CompilerParams(
            dimension_semantics=("arbitrary", "arbitrary"), vmem_limit_bytes=VMEM_LIMIT_BYTES),
    )(x2d, mixw, wrow, wt, nw, rope, wup, bup)


def _attn_kernel(qT_ref, k_ref, vT_ref, gate_ref, kb_ref, o_ref, w_ref, m_ref, acc_ref, sa_ref, sb_ref,
                 *, tq, kc, nchunks):
    assert nchunks % 2 == 0
    g = pl.program_id(1)
    nq = Q_PER_KV * tq
    qcat = jnp.concatenate([qT_ref[0, h * HEAD_DIM:(h + 1) * HEAD_DIM, :] for h in range(Q_PER_KV)], axis=1)
    zeros = jnp.zeros_like(qcat)
    w_ref[...] = jnp.where(g == 0, jnp.concatenate([qcat, zeros], axis=0), jnp.concatenate([zeros, qcat], axis=0))

    qf = qcat.astype(F32)
    m_ref[...] = jnp.sqrt(jnp.sum(qf * qf, axis=0, keepdims=True)) * kb_ref[...] - SHIFT_HEADROOM
    acc_ref[...] = jnp.zeros((V_ROWS, nq), F32)

    def key_chunk(c):
        start = c * kc if isinstance(c, int) else pl.multiple_of(c * kc, kc)
        return k_ref[pl.ds(start, kc), :]

    def step(c, s_cur, s_nxt):
        kch = None if s_nxt is None else key_chunk(c + 1)
        vch = None if s_cur is None else vT_ref[0, c]
        for j in range(nq // Q_COLS):
            cols = slice(j * Q_COLS, (j + 1) * Q_COLS)
            for r in range(kc // KEY_ROWS):
                rows = slice(r * KEY_ROWS, (r + 1) * KEY_ROWS)
                if s_nxt is not None:
                    s_nxt[rows, cols] = _dot(kch[rows], w_ref[:, cols])
                if s_cur is not None:
                    p = jnp.exp2(s_cur[rows, cols] - m_ref[:, cols]).astype(BF16)
                    acc_ref[:, cols] += _dot(vch[:, rows], p)

    def two_steps(i, carry):
        c = 2 * i
        step(c, sa_ref, sb_ref)
        step(c + 1, sb_ref, sa_ref)
        return carry

    step(-1, None, sa_ref)
    lax.fori_loop(0, nchunks // 2 - 1, two_steps, 0)
    step(nchunks - 2, sa_ref, sb_ref)
    step(nchunks - 1, sb_ref, None)

    @pl.when(jnp.logical_not(jnp.min(acc_ref[HEAD_DIM:HEAD_DIM + 1, :]) >= MIN_DENOMINATOR))
    def _():
        m_ref[...] = jnp.full((1, nq), NEG_BIG, F32)
        acc_ref[...] = jnp.zeros((V_ROWS, nq), F32)

        def running_max_body(c, carry):
            s = _dot(key_chunk(c), w_ref[...])
            m_old = m_ref[...]
            m_new = jnp.maximum(m_old, jnp.max(s, axis=0, keepdims=True))
            p = jnp.exp2(s - m_new).astype(BF16)
            acc_ref[...] = acc_ref[...] * jnp.exp2(m_old - m_new) + _dot(vT_ref[0, c], p)
            m_ref[...] = m_new
            return carry

        lax.fori_loop(0, nchunks, running_max_body, 0)

    acc = acc_ref[...]
    o = acc[0:HEAD_DIM] / acc[HEAD_DIM:HEAD_DIM + 1]
    oT = jnp.concatenate([o[:, h * tq:(h + 1) * tq] for h in range(Q_PER_KV)], axis=0)
    o_ref[...] = (oT.T * gate_ref[...]).astype(BF16)


def _attention(qT, k, vT, gate, kbound, B, T, tq, kc):
    nq_tiles = T // tq
    nchunks = T // kc
    gw = Q_PER_KV * HEAD_DIM
    nq = Q_PER_KV * tq
    kern = functools.partial(_attn_kernel, tq=tq, kc=kc, nchunks=nchunks)
    return pl.pallas_call(
        kern,
        grid=(B, N_KV_HEADS, nq_tiles),
        in_specs=[
            pl.BlockSpec((1, gw, tq), lambda b, g, i: (b, g, i)),
            pl.BlockSpec((T, KV_WIDTH), lambda b, g, i: (b, 0)),
            pl.BlockSpec((1, nchunks, V_ROWS, kc), lambda b, g, i: (b, 0, g, 0)),
            pl.BlockSpec((tq, gw), lambda b, g, i: (b * nq_tiles + i, g)),
            pl.BlockSpec((1, 1), lambda b, g, i: (0, 0)),
        ],
        out_specs=pl.BlockSpec((tq, gw), lambda b, g, i: (b * nq_tiles + i, g)),
        out_shape=jax.ShapeDtypeStruct((B * T, ATTN_WIDTH), BF16),
        scratch_shapes=[pltpu.VMEM((KV_WIDTH, nq), BF16), pltpu.VMEM((1, nq), F32),
                        pltpu.VMEM((V_ROWS, nq), F32), pltpu.VMEM((kc, nq), F32), pltpu.VMEM((kc, nq), F32)],
        name="attention",
        compiler_params=pltpu.CompilerParams(
            dimension_semantics=("arbitrary", "arbitrary", "arbitrary"), vmem_limit_bytes=VMEM_LIMIT_BYTES),
    )(qT, k, vT, gate, kbound)


def _gla_block(lq_ref, lv_ref, lkT_ref, gT_ref, s_ref, *, lg, reverse):
    nch = lg // CHUNK
    gT = gT_ref[0]
    t_src = lax.broadcasted_iota(jnp.int32, (lg, lg), 0)
    t_dst = lax.broadcasted_iota(jnp.int32, (lg, lg), 1)
    same = (t_src // CHUNK) == (t_dst // CHUNK)
    tri = jnp.where(same & ((t_src >= t_dst) if reverse else (t_src <= t_dst)), 1.0, 0.0).astype(BF16)
    g1 = gT.astype(BF16)
    r1 = gT - g1.astype(F32)
    g2 = r1.astype(BF16)
    g3 = (r1 - g2.astype(F32)).astype(BF16)
    bT = _dot(g1, tri) + _dot(g2, tri) + _dot(g3, tri)

    lane = lax.broadcasted_iota(jnp.int32, (GLA_KEY_WIDTH, lg), 1)
    tot_cols = []
    btot = jnp.zeros_like(bT)
    for n in range(nch):
        edge = n * CHUNK if reverse else n * CHUNK + CHUNK - 1
        col = bT[:, edge:edge + 1]
        tot_cols.append(col)
        btot = jnp.where((lane // CHUNK) == n, col, btot)

    kT = lkT_ref[0]
    k_decT = (kT * jnp.exp(-bT)).astype(BF16)
    k_kvT = (kT * jnp.exp(btot - bT)).astype(BF16)
    q_dec = lq_ref[...] * (GLA_DK ** -0.5) * jnp.exp(bT.T)
    v = lv_ref[...].astype(BF16)

    intra = same.T & ((t_src < t_dst) if reverse else (t_src >= t_dst))
    order = range(nch - 1, -1, -1) if reverse else range(nch)
    outs = []
    for h in range(GLA_HEADS):
        ks = slice(h * GLA_DK, (h + 1) * GLA_DK)
        vs = slice(h * GLA_DV, (h + 1) * GLA_DV)
        qh = q_dec[:, ks].astype(BF16)
        a = _dot(qh, k_decT[ks, :])
        a = jnp.where(intra, a, 0.0).astype(BF16)
        o_h = _dot(a, v[:, vs])
        inter = [None] * nch
        state = s_ref[h]
        for n in order:
            ts = slice(n * CHUNK, (n + 1) * CHUNK)
            inter[n] = _dot(qh[ts], state.astype(BF16))
            kv = _dot(k_kvT[ks, ts], v[ts, vs])
            state = jnp.exp(tot_cols[n][ks]) * state + kv
        s_ref[h] = state
        outs.append(o_h + jnp.concatenate(inter, axis=0))
    return jnp.concatenate(outs, axis=1)


def _gla_kernel(lq_f, lv_f, lkT_f, gT_f, lq_b, lv_b, lkT_b, gT_b, of_ref, ob_ref, sf_ref, sb_ref, *, lg):
    @pl.when(pl.program_id(1) == 0)
    def _():
        sf_ref[...] = jnp.zeros_like(sf_ref)
        sb_ref[...] = jnp.zeros_like(sb_ref)
    of_ref[...] = _gla_block(lq_f, lv_f, lkT_f, gT_f, sf_ref, lg=lg, reverse=False)
    ob_ref[...] = _gla_block(lq_b, lv_b, lkT_b, gT_b, sb_ref, lg=lg, reverse=True)


def _gla(lq, lv, lkT, gT, B, T, lg):
    nb = T // lg
    ftok = lambda b, i: (b * nb + i, 0)
    rtok = lambda b, i: (b * nb + (nb - 1 - i), 0)
    state = pltpu.VMEM((GLA_HEADS, GLA_DK, GLA_DV), F32)
    o_shape = jax.ShapeDtypeStruct((B * T, GLA_WIDTH), F32)
    return pl.pallas_call(
        functools.partial(_gla_kernel, lg=lg),
        grid=(B, nb),
        in_specs=[
            pl.BlockSpec((lg, GLA_KEY_WIDTH), ftok),
            pl.BlockSpec((lg, GLA_WIDTH), ftok),
            pl.BlockSpec((1, GLA_KEY_WIDTH, lg), lambda b, i: (b, 0, i)),
            pl.BlockSpec((1, GLA_KEY_WIDTH, lg), lambda b, i: (b, 0, i)),
            pl.BlockSpec((lg, GLA_KEY_WIDTH), rtok),
            pl.BlockSpec((lg, GLA_WIDTH), rtok),
            pl.BlockSpec((1, GLA_KEY_WIDTH, lg), lambda b, i: (b, 0, nb - 1 - i)),
            pl.BlockSpec((1, GLA_KEY_WIDTH, lg), lambda b, i: (b, 1, nb - 1 - i)),
        ],
        out_specs=(pl.BlockSpec((lg, GLA_WIDTH), ftok), pl.BlockSpec((lg, GLA_WIDTH), rtok)),
        out_shape=(o_shape, o_shape),
        scratch_shapes=[state, state], name="gla",
        compiler_params=pltpu.CompilerParams(dimension_semantics=("arbitrary", "arbitrary"),
                                             vmem_limit_bytes=VMEM_LIMIT_BYTES),
    )(lq, lv, lkT, gT, lq, lv, lkT, gT)


def _out_kernel(x_ref, ma_ref, of_ref, ob_ref, lgate_ref, glaw_ref, p_ref, wo_ref, plew_ref, wg_ref, wp_ref,
                fw_ref, y_ref, *, final):
    o = of_ref[...] + ob_ref[...]
    lgate = lgate_ref[...]
    mg = []
    for hd in range(GLA_HEADS):
        vs = slice(hd * GLA_DV, (hd + 1) * GLA_DV)
        oh = o[:, vs]
        y = oh * lax.rsqrt(jnp.mean(oh * oh, axis=-1, keepdims=True) + EPS) * glaw_ref[...]
        mg.append((y * lgate[:, vs]).astype(BF16))
    mg = jnp.concatenate(mg, axis=1)
    h = x_ref[...] + _dot(ma_ref[...], wo_ref[0:ATTN_WIDTH, :]) + _dot(mg, wo_ref[ATTN_WIDTH:, :])
    hn = h * lax.rsqrt(jnp.mean(h * h, axis=-1, keepdims=True) + EPS) * plew_ref[...]
    gate = jax.nn.sigmoid(_dot(hn.astype(BF16), wg_ref[...]))
    h = h + gate * _dot(p_ref[...].astype(BF16), wp_ref[...])
    if final:
        h = h * lax.rsqrt(jnp.mean(h * h, axis=-1, keepdims=True) + EPS) * fw_ref[...]
    y_ref[...] = h


def _outproj(x2d, ma, o_fwd, o_bwd, lgate, glaw, p2d, wo, plew, wg, wp, fw, tm, final):
    n = x2d.shape[0]
    tok = lambda i: (i, 0)
    const = lambda i: (0, 0)
    return pl.pallas_call(
        functools.partial(_out_kernel, final=final),
        grid=(n // tm,),
        in_specs=[
            pl.BlockSpec((tm, D_MODEL), tok),
            pl.BlockSpec((tm, ATTN_WIDTH), tok),
            pl.BlockSpec((tm, GLA_WIDTH), tok),
            pl.BlockSpec((tm, GLA_WIDTH), tok),
            pl.BlockSpec((tm, GLA_WIDTH), tok),
            pl.BlockSpec(glaw.shape, const),
            pl.BlockSpec((tm, PLE_DIM), tok),
            pl.BlockSpec(wo.shape, const),
            pl.BlockSpec(plew.shape, const),
            pl.BlockSpec(wg.shape, const),
            pl.BlockSpec(wp.shape, const),
            pl.BlockSpec(fw.shape, const),
        ],
        out_specs=pl.BlockSpec((tm, D_MODEL), tok),
        out_shape=jax.ShapeDtypeStruct((n, D_MODEL), F32),
        name="outproj",
        compiler_params=pltpu.CompilerParams(
            dimension_semantics=("arbitrary",), vmem_limit_bytes=VMEM_LIMIT_BYTES),
    )(x2d, ma, o_fwd, o_bwd, lgate, glaw, p2d, wo, plew, wg, wp, fw)


def _rope_tables(T):
    t = jnp.arange(T, dtype=jnp.int32)
    row = (t // GRID_W).astype(F32)
    col = (t % GRID_W).astype(F32)
    half = HEAD_DIM // 2
    inv_freq = ROPE_THETA ** (-jnp.arange(0, half, 2, dtype=F32) / half)
    ang_r = inv_freq[:, None] * row[None, :]
    ang_c = inv_freq[:, None] * col[None, :]
    return jnp.stack([jnp.cos(ang_r), jnp.sin(ang_r), jnp.cos(ang_c), jnp.sin(ang_c)])


def _prep_layer_weights(mix_norm, w_in, q_norm, k_norm, w_up_f, b_f, w_up_b, b_b, gla_norm, w_out,
                        ple_norm, w_ple_gate, w_ple_proj):
    c = [0, 512, 640, 768, 1280, 1536, 1792, 2304, 2816, 2832, 2848]
    seg = lambda j: w_in[:, c[j]:c[j + 1]]
    a_q, a_k, a_v, a_gate, l_q, l_k, l_v, l_gate, lr_f, lr_b = (seg(j) for j in range(10))
    wrow = jnp.concatenate([a_gate, l_q, l_v, l_gate], axis=1).astype(BF16)
    wt = jnp.concatenate([a_q, a_k, a_v, l_k, lr_f, lr_b], axis=1).T.astype(BF16)
    qs = q_norm * (HEAD_DIM ** -0.5 * LOG2E)
    nw = jnp.concatenate([jnp.tile(qs[None], (N_Q_HEADS, 1)), jnp.tile(k_norm[None], (N_KV_HEADS, 1))])[..., None]
    zero = jnp.zeros((GLA_KEY_WIDTH, GATE_RANK), F32)
    wup = jnp.concatenate([jnp.concatenate([w_up_f.T, zero], axis=1),
                           jnp.concatenate([zero, w_up_b.T], axis=1)], axis=0).astype(BF16)
    bup = jnp.concatenate([b_f, b_b])[:, None]
    kbound = (jnp.max(jnp.abs(k_norm)) * (HEAD_DIM ** 0.5 * (1.0 + 2.0 ** -7))).reshape(1, 1)
    return dict(mixw=mix_norm[None], wrow=wrow, wt=wt, nw=nw, wup=wup, bup=bup, kbound=kbound,
                gla_nw=gla_norm[None],
                wo=w_out.astype(BF16), plew=ple_norm[None], wg=w_ple_gate.astype(BF16),
                wp=w_ple_proj.astype(BF16))


def _layer(h2d, p2d, B, T, w, fw, final, tiles):
    tm, tq, kc, lg, tmo = tiles
    rope = _rope_tables(T)
    qT, k, vT, ag, lq, lv, lgate, lkT, gT = _inproj(
        h2d, B, T, tm, kc, w["mixw"], w["wrow"], w["wt"], w["nw"], rope, w["wup"], w["bup"])
    ma = _attention(qT, k, vT, ag, w["kbound"], B, T, tq, kc)
    o_fwd, o_bwd = _gla(lq, lv, lkT, gT, B, T, lg)
    return _outproj(h2d, ma, o_fwd, o_bwd, lgate, w["gla_nw"], p2d, w["wo"], w["plew"], w["wg"], w["wp"], fw,
                    tmo, final)


def _pick_tiles(T):
    return (min(512, T), min(256, T), min(1024, T), min(256, T), min(512, T))


def kernel(x_prompt, x_sample, p_prompt, p_sample, mix_norm, w_in, q_norm, k_norm, w_gate_up_fwd,
           b_gate_fwd, w_gate_up_bwd, b_gate_bwd, gla_norm, w_out, ple_norm, w_ple_gate, w_ple_proj,
           final_norm):
    depth = w_in.shape[0]
    fw = final_norm[None]
    outs = []
    for x, p in ((x_prompt, p_prompt), (x_sample, p_sample)):
        B, T, _ = x.shape
        h = x.reshape(B * T, D_MODEL)
        for i in range(depth):
            w = _prep_layer_weights(mix_norm[i], w_in[i], q_norm[i], k_norm[i], w_gate_up_fwd[i],
                                    b_gate_fwd[i], w_gate_up_bwd[i], b_gate_bwd[i], gla_norm[i], w_out[i],
                                    ple_norm[i], w_ple_gate[i], w_ple_proj[i])
            h = _layer(h, p[i].reshape(B * T, PLE_DIM), B, T, w, fw, i == depth - 1, _pick_tiles(T))
        outs.append(h.reshape(B, T, D_MODEL))
    return tuple(outs)
```

```python
import functools
import math

import jax
import jax.numpy as jnp
from jax import lax
from jax.experimental import pallas as pl
from jax.experimental.pallas import tpu as pltpu

F32 = jnp.float32
BF16 = jnp.bfloat16

D_MODEL = 1024
GRID_W = 64
N_Q_HEADS = 8
N_KV_HEADS = 2
HEAD_DIM = 64
Q_PER_KV = N_Q_HEADS // N_KV_HEADS
ATTN_WIDTH = N_Q_HEADS * HEAD_DIM
KV_WIDTH = N_KV_HEADS * HEAD_DIM
ROPE_THETA = 10000.0
GLA_HEADS = 4
GLA_DK = 64
GLA_DV = 128
GLA_KEY_WIDTH = GLA_HEADS * GLA_DK
GLA_WIDTH = GLA_HEADS * GLA_DV
GATE_RANK = 16
GATE_NORMALIZER = 16.0
CHUNK = 64
PLE_DIM = 256
EPS = 1e-6
LOG2E = math.log2(math.e)

VMEM_LIMIT_BYTES = 56 * 1024 * 1024
BF16_SUBLANES = 16
V_ROWS = HEAD_DIM + BF16_SUBLANES
NEG_BIG = -1e30
Q_COLS = 512
KEY_ROWS = 256
SHIFT_HEADROOM = 100.0
MIN_DENOMINATOR = 2.0 ** -60
TOKEN_SUB = 256
GLA_SUB_BLOCKS = 4

NT_DIMS = (((1,), (1,)), ((), ()))


def _dot(a, b):
    return jnp.dot(a, b, preferred_element_type=F32)


def _silu(x):
    return x * jax.nn.sigmoid(x)


def _inproj_kernel(x_ref, mixw_ref, wrow_ref, wt_ref, nw_ref, rope_ref, wup_ref, bup_ref,
                   qT_ref, k_ref, vT_ref, ag_ref, lq_ref, lv_ref, lg_ref, lkT_ref, gT_ref):
    subs = [slice(j * TOKEN_SUB, (j + 1) * TOKEN_SUB) for j in range(x_ref.shape[0] // TOKEN_SUB)]
    xns, zts = [], []
    for ts in subs:
        x = x_ref[ts, :]
        xn = (x * lax.rsqrt(jnp.mean(x * x, axis=-1, keepdims=True) + EPS) * mixw_ref[...]).astype(BF16)
        xns.append(xn)
        zts.append(lax.dot_general(wt_ref[...], xn, NT_DIMS, preferred_element_type=F32))
    xn = jnp.concatenate(xns, axis=0)

    def norm_rope(zh, w, ts):
        cr, sr, cc, sc = rope_ref[0, :, ts], rope_ref[1, :, ts], rope_ref[2, :, ts], rope_ref[3, :, ts]
        y = zh * lax.rsqrt(jnp.mean(zh * zh, axis=0, keepdims=True) + EPS) * w
        r1, r2, c1, c2 = y[0:16], y[16:32], y[32:48], y[48:64]
        return jnp.concatenate([r1 * cr - r2 * sr, r2 * cr + r1 * sr,
                                c1 * cc - c2 * sc, c2 * cc + c1 * sc], axis=0)

    def q_epilogue(ts, zt):
        for h in range(N_Q_HEADS):
            qh = norm_rope(zt[h * HEAD_DIM:(h + 1) * HEAD_DIM], nw_ref[h], ts)
            qT_ref[0, h * HEAD_DIM:(h + 1) * HEAD_DIM, ts] = qh.astype(BF16)

    def kv_gla_epilogue(ts, zt):
        kT = jnp.concatenate(
            [norm_rope(zt[ATTN_WIDTH + h * HEAD_DIM:ATTN_WIDTH + (h + 1) * HEAD_DIM], nw_ref[N_Q_HEADS + h], ts)
             for h in range(N_KV_HEADS)], axis=0)
        k_ref[ts, :] = kT.T.astype(BF16)
        ones = jnp.ones((BF16_SUBLANES, TOKEN_SUB), F32)
        v0 = ATTN_WIDTH + KV_WIDTH
        vT = jnp.concatenate([zt[v0:v0 + HEAD_DIM], ones, zt[v0 + HEAD_DIM:v0 + 2 * HEAD_DIM], ones], axis=0)
        vT_ref[0, 0, :, ts] = vT.astype(BF16)
        l0 = v0 + KV_WIDTH
        lkT_ref[0, :, ts] = zt[l0:l0 + GLA_KEY_WIDTH]
        lrT = zt[l0 + GLA_KEY_WIDTH:l0 + GLA_KEY_WIDTH + 2 * GATE_RANK].astype(BF16)
        gpre = _dot(wup_ref[...], lrT) + bup_ref[...]
        gT_ref[0, :, ts] = jax.nn.log_sigmoid(gpre) * (1.0 / GATE_NORMALIZER)

    feature_major_epilogues = [functools.partial(f, ts, zt) for ts, zt in zip(subs, zts)
                               for f in (q_epilogue, kv_gla_epilogue)]
    groups = [(0, 512, ag_ref, _silu), (512, 768, lq_ref, None), (768, 1280, lv_ref, None),
              (1280, 1792, lg_ref, _silu)]
    token_major = []

    def store(gi):
        _, _, out_ref, act = groups[gi]
        out_ref[...] = token_major[gi] if act is None else act(token_major[gi])

    for gi, (c0, c1, _, _) in enumerate(groups):
        token_major.append(_dot(xn, wrow_ref[:, c0:c1]))
        if gi > 0:
            store(gi - 1)
        for epilogue in feature_major_epilogues[gi::len(groups)]:
            epilogue()
    store(len(groups) - 1)


def _inproj(x2d, B, T, tm, kc, mixw, wrow, wt, nw, rope, wup, bup):
    nt = T // tm
    per_kc = kc // tm
    grid = (B, nt)
    tok = lambda b, i: (b * nt + i, 0)
    fm = lambda b, i: (b, 0, i)
    const2 = lambda b, i: (0, 0)
    const3 = lambda b, i: (0, 0, 0)
    out_shape = (
        jax.ShapeDtypeStruct((B, ATTN_WIDTH, T), BF16),
        jax.ShapeDtypeStruct((B * T, KV_WIDTH), BF16),
        jax.ShapeDtypeStruct((B, T // kc, N_KV_HEADS * V_ROWS, kc), BF16),
        jax.ShapeDtypeStruct((B * T, ATTN_WIDTH), F32),
        jax.ShapeDtypeStruct((B * T, GLA_KEY_WIDTH), F32),
        jax.ShapeDtypeStruct((B * T, GLA_WIDTH), F32),
        jax.ShapeDtypeStruct((B * T, GLA_WIDTH), F32),
        jax.ShapeDtypeStruct((B, GLA_KEY_WIDTH, T), F32),
        jax.ShapeDtypeStruct((B, 2 * GLA_KEY_WIDTH, T), F32),
    )
    out_specs = (
        pl.BlockSpec((1, ATTN_WIDTH, tm), fm),
        pl.BlockSpec((tm, KV_WIDTH), tok),
        pl.BlockSpec((1, 1, N_KV_HEADS * V_ROWS, tm), lambda b, i: (b, i // per_kc, 0, i % per_kc)),
        pl.BlockSpec((tm, ATTN_WIDTH), tok),
        pl.BlockSpec((tm, GLA_KEY_WIDTH), tok),
        pl.BlockSpec((tm, GLA_WIDTH), tok),
        pl.BlockSpec((tm, GLA_WIDTH), tok),
        pl.BlockSpec((1, GLA_KEY_WIDTH, tm), fm),
        pl.BlockSpec((1, 2 * GLA_KEY_WIDTH, tm), fm),
    )
    in_specs = [
        pl.BlockSpec((tm, D_MODEL), tok),
        pl.BlockSpec(mixw.shape, const2),
        pl.BlockSpec(wrow.shape, const2),
        pl.BlockSpec(wt.shape, const2),
        pl.BlockSpec(nw.shape, const3),
        pl.BlockSpec((4, HEAD_DIM // 4, tm), lambda b, i: (0, 0, i)),
        pl.BlockSpec(wup.shape, const2),
        pl.BlockSpec(bup.shape, const2),
    ]
    return pl.pallas_call(
        _inproj_kernel, grid=grid, in_specs=in_specs, out_specs=out_specs, out_shape=out_shape,
        name="inproj",
        compiler_params=pltpu.CompilerParams(
            dimension_semantics=("arbitrary", "arbitrary"), vmem_limit_bytes=VMEM_LIMIT_BYTES),
    )(x2d, mixw, wrow, wt, nw, rope, wup, bup)


def _attn_kernel(qT_ref, k_ref, vT_ref, gate_ref, kb_ref, o_ref, w_ref, m_ref, acc_ref, sa_ref, sb_ref,
                 *, tq, kc, nchunks):
    assert nchunks % 2 == 0
    g = pl.program_id(1)
    nq = Q_PER_KV * tq
    qcat = jnp.concatenate([qT_ref[0, h * HEAD_DIM:(h + 1) * HEAD_DIM, :] for h in range(Q_PER_KV)], axis=1)
    zeros = jnp.zeros_like(qcat)
    w_ref[...] = jnp.where(g == 0, jnp.concatenate([qcat, zeros], axis=0), jnp.concatenate([zeros, qcat], axis=0))

    qf = qcat.astype(F32)
    m_ref[...] = jnp.sqrt(jnp.sum(qf * qf, axis=0, keepdims=True)) * kb_ref[...] - SHIFT_HEADROOM
    acc_ref[...] = jnp.zeros((V_ROWS, nq), F32)

    def key_chunk(c):
        start = c * kc if isinstance(c, int) else pl.multiple_of(c * kc, kc)
        return k_ref[pl.ds(start, kc), :]

    def step(c, s_cur, s_nxt):
        kch = None if s_nxt is None else key_chunk(c + 1)
        vch = None if s_cur is None else vT_ref[0, c]
        for j in range(nq // Q_COLS):
            cols = slice(j * Q_COLS, (j + 1) * Q_COLS)
            for r in range(kc // KEY_ROWS):
                rows = slice(r * KEY_ROWS, (r + 1) * KEY_ROWS)
                if s_nxt is not None:
                    s_nxt[rows, cols] = _dot(kch[rows], w_ref[:, cols])
                if s_cur is not None:
                    p = jnp.exp2(s_cur[rows, cols] - m_ref[:, cols]).astype(BF16)
                    acc_ref[:, cols] += _dot(vch[:, rows], p)

    def two_steps(i, carry):
        c = 2 * i
        step(c, sa_ref, sb_ref)
        step(c + 1, sb_ref, sa_ref)
        return carry

    step(-1, None, sa_ref)
    lax.fori_loop(0, nchunks // 2 - 1, two_steps, 0)
    step(nchunks - 2, sa_ref, sb_ref)
    step(nchunks - 1, sb_ref, None)

    @pl.when(jnp.logical_not(jnp.min(acc_ref[HEAD_DIM:HEAD_DIM + 1, :]) >= MIN_DENOMINATOR))
    def _():
        m_ref[...] = jnp.full((1, nq), NEG_BIG, F32)
        acc_ref[...] = jnp.zeros((V_ROWS, nq), F32)

        def running_max_body(c, carry):
            s = _dot(key_chunk(c), w_ref[...])
            m_old = m_ref[...]
            m_new = jnp.maximum(m_old, jnp.max(s, axis=0, keepdims=True))
            p = jnp.exp2(s - m_new).astype(BF16)
            acc_ref[...] = acc_ref[...] * jnp.exp2(m_old - m_new) + _dot(vT_ref[0, c], p)
            m_ref[...] = m_new
            return carry

        lax.fori_loop(0, nchunks, running_max_body, 0)

    acc = acc_ref[...]
    o = acc[0:HEAD_DIM] / acc[HEAD_DIM:HEAD_DIM + 1]
    oT = jnp.concatenate([o[:, h * tq:(h + 1) * tq] for h in range(Q_PER_KV)], axis=0)
    o_ref[...] = (oT.T * gate_ref[...]).astype(BF16)


def _attention(qT, k, vT, gate, kbound, B, T, tq, kc):
    nq_tiles = T // tq
    nchunks = T // kc
    gw = Q_PER_KV * HEAD_DIM
    nq = Q_PER_KV * tq
    kern = functools.partial(_attn_kernel, tq=tq, kc=kc, nchunks=nchunks)
    return pl.pallas_call(
        kern,
        grid=(B, N_KV_HEADS, nq_tiles),
        in_specs=[
            pl.BlockSpec((1, gw, tq), lambda b, g, i: (b, g, i)),
            pl.BlockSpec((T, KV_WIDTH), lambda b, g, i: (b, 0)),
            pl.BlockSpec((1, nchunks, V_ROWS, kc), lambda b, g, i: (b, 0, g, 0)),
            pl.BlockSpec((tq, gw), lambda b, g, i: (b * nq_tiles + i, g)),
            pl.BlockSpec((1, 1), lambda b, g, i: (0, 0)),
        ],
        out_specs=pl.BlockSpec((tq, gw), lambda b, g, i: (b * nq_tiles + i, g)),
        out_shape=jax.ShapeDtypeStruct((B * T, ATTN_WIDTH), BF16),
        scratch_shapes=[pltpu.VMEM((KV_WIDTH, nq), BF16), pltpu.VMEM((1, nq), F32),
                        pltpu.VMEM((V_ROWS, nq), F32), pltpu.VMEM((kc, nq), F32), pltpu.VMEM((kc, nq), F32)],
        name="attention",
        compiler_params=pltpu.CompilerParams(
            dimension_semantics=("arbitrary", "arbitrary", "arbitrary"), vmem_limit_bytes=VMEM_LIMIT_BYTES),
    )(qT, k, vT, gate, kbound)


def _gla_chains(chains, lg):
    nch = lg // CHUNK
    t_src = lax.broadcasted_iota(jnp.int32, (lg, lg), 0)
    t_dst = lax.broadcasted_iota(jnp.int32, (lg, lg), 1)
    same = (t_src // CHUNK) == (t_dst // CHUNK)
    lane = lax.broadcasted_iota(jnp.int32, (GLA_KEY_WIDTH, lg), 1)
    heads = [(slice(h * GLA_DK, (h + 1) * GLA_DK), slice(h * GLA_DV, (h + 1) * GLA_DV)) for h in range(GLA_HEADS)]

    bTs = []
    for (_, _, _, gT, _, reverse) in chains:
        tri = jnp.where(same & ((t_src >= t_dst) if reverse else (t_src <= t_dst)), 1.0, 0.0).astype(BF16)
        g1 = gT.astype(BF16)
        r1 = gT - g1.astype(F32)
        g2 = r1.astype(BF16)
        g3 = (r1 - g2.astype(F32)).astype(BF16)
        bTs.append(_dot(g1, tri) + _dot(g2, tri) + _dot(g3, tri))

    prep = []
    for (lq, lv, kT, _, _, reverse), bT in zip(chains, bTs):
        tot_cols = []
        btot = jnp.zeros_like(bT)
        for n in range(nch):
            edge = n * CHUNK if reverse else n * CHUNK + CHUNK - 1
            col = bT[:, edge:edge + 1]
            tot_cols.append(jnp.exp(col))
            btot = jnp.where((lane // CHUNK) == n, col, btot)
        k_decT = (kT * jnp.exp(-bT)).astype(BF16)
        k_kvT = (kT * jnp.exp(btot - bT)).astype(BF16)
        q_dec = (lq * (GLA_DK ** -0.5) * jnp.exp(bT.T)).astype(BF16)
        v = lv.astype(BF16)
        intra = same & ((t_src < t_dst) if reverse else (t_src >= t_dst))
        prep.append((q_dec, k_decT, k_kvT, v, tot_cols, intra))

    a = [[jnp.where(intra, _dot(q_dec[:, ks], k_decT[ks, :]), 0.0).astype(BF16) for ks, _ in heads]
         for (q_dec, k_decT, _, _, _, intra) in prep]
    kv = [[[_dot(k_kvT[ks, n * CHUNK:(n + 1) * CHUNK], v[n * CHUNK:(n + 1) * CHUNK, vs]) for n in range(nch)]
           for ks, vs in heads] for (_, _, k_kvT, v, _, _) in prep]
    o = [[_dot(a[ci][h], v[:, vs]) for h, (_, vs) in enumerate(heads)]
         for ci, (_, _, _, v, _, _) in enumerate(prep)]

    outs = []
    for ci, ((_, _, _, _, s_ref, reverse), (q_dec, _, _, _, tot_cols, _)) in enumerate(zip(chains, prep)):
        order = range(nch - 1, -1, -1) if reverse else range(nch)
        per_head = []
        for h, (ks, _) in enumerate(heads):
            inter = [None] * nch
            state = s_ref[h]
            for n in order:
                inter[n] = _dot(q_dec[n * CHUNK:(n + 1) * CHUNK, ks], state.astype(BF16))
                state = tot_cols[n][ks] * state + kv[ci][h][n]
            s_ref[h] = state
            per_head.append(o[ci][h] + jnp.concatenate(inter, axis=0))
        outs.append(jnp.concatenate(per_head, axis=1))
    return outs


def _gla_kernel(lq_f, lv_f, lkT_f, gT_f, lq_b, lv_b, lkT_b, gT_b, of_ref, ob_ref, sf_ref, sb_ref, *, lg, nsub):
    @pl.when(pl.program_id(1) == 0)
    def _():
        sf_ref[...] = jnp.zeros_like(sf_ref)
        sb_ref[...] = jnp.zeros_like(sb_ref)

    def sub(refs, j, s_ref, reverse):
        lq_ref, lv_ref, lkT_ref, gT_ref = refs
        ts = slice(j * lg, (j + 1) * lg)
        return (lq_ref[ts, :], lv_ref[ts, :], lkT_ref[0, :, ts], gT_ref[0, :, ts], s_ref, reverse)

    fwd = [sub((lq_f, lv_f, lkT_f, gT_f), j, sf_ref, False) for j in range(nsub)]
    bwd = [sub((lq_b, lv_b, lkT_b, gT_b), j, sb_ref, True) for j in range(nsub - 1, -1, -1)]
    outs = _gla_chains(fwd + bwd, lg)
    for j in range(nsub):
        of_ref[j * lg:(j + 1) * lg, :] = outs[j]
        ob_ref[(nsub - 1 - j) * lg:(nsub - j) * lg, :] = outs[nsub + j]


def _gla(lq, lv, lkT, gT, B, T, lg, nsub):
    blk = lg * nsub
    nb = T // blk
    ftok = lambda b, i: (b * nb + i, 0)
    rtok = lambda b, i: (b * nb + (nb - 1 - i), 0)
    state = pltpu.VMEM((GLA_HEADS, GLA_DK, GLA_DV), F32)
    o_shape = jax.ShapeDtypeStruct((B * T, GLA_WIDTH), F32)
    return pl.pallas_call(
        functools.partial(_gla_kernel, lg=lg, nsub=nsub),
        grid=(B, nb),
        in_specs=[
            pl.BlockSpec((blk, GLA_KEY_WIDTH), ftok),
            pl.BlockSpec((blk, GLA_WIDTH), ftok),
            pl.BlockSpec((1, GLA_KEY_WIDTH, blk), lambda b, i: (b, 0, i)),
            pl.BlockSpec((1, GLA_KEY_WIDTH, blk), lambda b, i: (b, 0, i)),
            pl.BlockSpec((blk, GLA_KEY_WIDTH), rtok),
            pl.BlockSpec((blk, GLA_WIDTH), rtok),
            pl.BlockSpec((1, GLA_KEY_WIDTH, blk), lambda b, i: (b, 0, nb - 1 - i)),
            pl.BlockSpec((1, GLA_KEY_WIDTH, blk), lambda b, i: (b, 1, nb - 1 - i)),
        ],
        out_specs=(pl.BlockSpec((blk, GLA_WIDTH), ftok), pl.BlockSpec((blk, GLA_WIDTH), rtok)),
        out_shape=(o_shape, o_shape),
        scratch_shapes=[state, state], name="gla",
        compiler_params=pltpu.CompilerParams(dimension_semantics=("arbitrary", "arbitrary"),
                                             vmem_limit_bytes=VMEM_LIMIT_BYTES),
    )(lq, lv, lkT, gT, lq, lv, lkT, gT)


def _out_kernel(x_ref, ma_ref, of_ref, ob_ref, lgate_ref, glaw_ref, p_ref, wo_ref, plew_ref, wg_ref, wp_ref,
                fw_ref, y_ref, *, final):
    subs = [slice(j * TOKEN_SUB, (j + 1) * TOKEN_SUB) for j in range(x_ref.shape[0] // TOKEN_SUB)]
    ha = [x_ref[ts, :] + _dot(ma_ref[ts, :], wo_ref[0:ATTN_WIDTH, :]) for ts in subs]
    pe = [_dot(p_ref[ts, :].astype(BF16), wp_ref[...]) for ts in subs]
    hs = []
    for ts, h in zip(subs, ha):
        o = of_ref[ts, :] + ob_ref[ts, :]
        lgate = lgate_ref[ts, :]
        mg = []
        for hd in range(GLA_HEADS):
            vs = slice(hd * GLA_DV, (hd + 1) * GLA_DV)
            oh = o[:, vs]
            y = oh * lax.rsqrt(jnp.mean(oh * oh, axis=-1, keepdims=True) + EPS) * glaw_ref[...]
            mg.append((y * lgate[:, vs]).astype(BF16))
        hs.append(h + _dot(jnp.concatenate(mg, axis=1), wo_ref[ATTN_WIDTH:, :]))
    gates = []
    for h in hs:
        hn = h * lax.rsqrt(jnp.mean(h * h, axis=-1, keepdims=True) + EPS) * plew_ref[...]
        gates.append(_dot(hn.astype(BF16), wg_ref[...]))
    for ts, h, g, e in zip(subs, hs, gates, pe):
        h = h + jax.nn.sigmoid(g) * e
        if final:
            h = h * lax.rsqrt(jnp.mean(h * h, axis=-1, keepdims=True) + EPS) * fw_ref[...]
        y_ref[ts, :] = h


def _outproj(x2d, ma, o_fwd, o_bwd, lgate, glaw, p2d, wo, plew, wg, wp, fw, tm, final):
    n = x2d.shape[0]
    tok = lambda i: (i, 0)
    const = lambda i: (0, 0)
    return pl.pallas_call(
        functools.partial(_out_kernel, final=final),
        grid=(n // tm,),
        in_specs=[
            pl.BlockSpec((tm, D_MODEL), tok),
            pl.BlockSpec((tm, ATTN_WIDTH), tok),
            pl.BlockSpec((tm, GLA_WIDTH), tok),
            pl.BlockSpec((tm, GLA_WIDTH), tok),
            pl.BlockSpec((tm, GLA_WIDTH), tok),
            pl.BlockSpec(glaw.shape, const),
            pl.BlockSpec((tm, PLE_DIM), tok),
            pl.BlockSpec(wo.shape, const),
            pl.BlockSpec(plew.shape, const),
            pl.BlockSpec(wg.shape, const),
            pl.BlockSpec(wp.shape, const),
            pl.BlockSpec(fw.shape, const),
        ],
        out_specs=pl.BlockSpec((tm, D_MODEL), tok),
        out_shape=jax.ShapeDtypeStruct((n, D_MODEL), F32),
        name="outproj",
        compiler_params=pltpu.CompilerParams(
            dimension_semantics=("arbitrary",), vmem_limit_bytes=VMEM_LIMIT_BYTES),
    )(x2d, ma, o_fwd, o_bwd, lgate, glaw, p2d, wo, plew, wg, wp, fw)


def _rope_tables(T):
    t = jnp.arange(T, dtype=jnp.int32)
    row = (t // GRID_W).astype(F32)
    col = (t % GRID_W).astype(F32)
    half = HEAD_DIM // 2
    inv_freq = ROPE_THETA ** (-jnp.arange(0, half, 2, dtype=F32) / half)
    ang_r = inv_freq[:, None] * row[None, :]
    ang_c = inv_freq[:, None] * col[None, :]
    return jnp.stack([jnp.cos(ang_r), jnp.sin(ang_r), jnp.cos(ang_c), jnp.sin(ang_c)])


def _prep_layer_weights(mix_norm, w_in, q_norm, k_norm, w_up_f, b_f, w_up_b, b_b, gla_norm, w_out,
                        ple_norm, w_ple_gate, w_ple_proj):
    c = [0, 512, 640, 768, 1280, 1536, 1792, 2304, 2816, 2832, 2848]
    seg = lambda j: w_in[:, c[j]:c[j + 1]]
    a_q, a_k, a_v, a_gate, l_q, l_k, l_v, l_gate, lr_f, lr_b = (seg(j) for j in range(10))
    wrow = jnp.concatenate([a_gate, l_q, l_v, l_gate], axis=1).astype(BF16)
    wt = jnp.concatenate([a_q, a_k, a_v, l_k, lr_f, lr_b], axis=1).T.astype(BF16)
    qs = q_norm * (HEAD_DIM ** -0.5 * LOG2E)
    nw = jnp.concatenate([jnp.tile(qs[None], (N_Q_HEADS, 1)), jnp.tile(k_norm[None], (N_KV_HEADS, 1))])[..., None]
    zero = jnp.zeros((GLA_KEY_WIDTH, GATE_RANK), F32)
    wup = jnp.concatenate([jnp.concatenate([w_up_f.T, zero], axis=1),
                           jnp.concatenate([zero, w_up_b.T], axis=1)], axis=0).astype(BF16)
    bup = jnp.concatenate([b_f, b_b])[:, None]
    kbound = (jnp.max(jnp.abs(k_norm)) * (HEAD_DIM ** 0.5 * (1.0 + 2.0 ** -7))).reshape(1, 1)
    return dict(mixw=mix_norm[None], wrow=wrow, wt=wt, nw=nw, wup=wup, bup=bup, kbound=kbound,
                gla_nw=gla_norm[None],
                wo=w_out.astype(BF16), plew=ple_norm[None], wg=w_ple_gate.astype(BF16),
                wp=w_ple_proj.astype(BF16))


def _layer(h2d, p2d, B, T, w, fw, final, tiles):
    tm, tq, kc, lg, tmo = tiles
    rope = _rope_tables(T)
    qT, k, vT, ag, lq, lv, lgate, lkT, gT = _inproj(
        h2d, B, T, tm, kc, w["mixw"], w["wrow"], w["wt"], w["nw"], rope, w["wup"], w["bup"])
    ma = _attention(qT, k, vT, ag, w["kbound"], B, T, tq, kc)
    o_fwd, o_bwd = _gla(lq, lv, lkT, gT, B, T, lg, GLA_SUB_BLOCKS)
    return _outproj(h2d, ma, o_fwd, o_bwd, lgate, w["gla_nw"], p2d, w["wo"], w["plew"], w["wg"], w["wp"], fw,
                    tmo, final)


def _pick_tiles(T):
    return (min(512, T), min(256, T), min(1024, T), min(256, T), min(512, T))


def kernel(x_prompt, x_sample, p_prompt, p_sample, mix_norm, w_in, q_norm, k_norm, w_gate_up_fwd,
           b_gate_fwd, w_gate_up_bwd, b_gate_bwd, gla_norm, w_out, ple_norm, w_ple_gate, w_ple_proj,
           final_norm):
    depth = w_in.shape[0]
    fw = final_norm[None]
    outs = []
    for x, p in ((x_prompt, p_prompt), (x_sample, p_sample)):
        B, T, _ = x.shape
        h = x.reshape(B * T, D_MODEL)
        for i in range(depth):
            w = _prep_layer_weights(mix_norm[i], w_in[i], q_norm[i], k_norm[i], w_gate_up_fwd[i],
                                    b_gate_fwd[i], w_gate_up_bwd[i], b_gate_bwd[i], gla_norm[i], w_out[i],
                                    ple_norm[i], w_ple_gate[i], w_ple_proj[i])
            h = _layer(h, p[i].reshape(B * T, PLE_DIM), B, T, w, fw, i == depth - 1, _pick_tiles(T))
        outs.append(h.reshape(B, T, D_MODEL))
    return tuple(outs)
```

```python
import functools
import math

import jax
import jax.numpy as jnp
from jax import lax
from jax.experimental import pallas as pl
from jax.experimental.pallas import tpu as pltpu

F32 = jnp.float32
BF16 = jnp.bfloat16

D_MODEL = 1024
GRID_W = 64
N_Q_HEADS = 8
N_KV_HEADS = 2
HEAD_DIM = 64
Q_PER_KV = N_Q_HEADS // N_KV_HEADS
ATTN_WIDTH = N_Q_HEADS * HEAD_DIM
KV_WIDTH = N_KV_HEADS * HEAD_DIM
ROPE_THETA = 10000.0
GLA_HEADS = 4
GLA_DK = 64
GLA_DV = 128
GLA_KEY_WIDTH = GLA_HEADS * GLA_DK
GLA_WIDTH = GLA_HEADS * GLA_DV
GATE_RANK = 16
GATE_NORMALIZER = 16.0
CHUNK = 64
PLE_DIM = 256
EPS = 1e-6
LOG2E = math.log2(math.e)

VMEM_LIMIT_BYTES = 56 * 1024 * 1024
BF16_SUBLANES = 16
V_ROWS = HEAD_DIM + BF16_SUBLANES
NEG_BIG = -1e30
Q_COLS = 512
KEY_ROWS = 256
SHIFT_HEADROOM = 100.0
MIN_DENOMINATOR = 2.0 ** -60
TOKEN_SUB = 256
GLA_SUB_BLOCKS = 4

NT_DIMS = (((1,), (1,)), ((), ()))


def _dot(a, b):
    return jnp.dot(a, b, preferred_element_type=F32)


def _silu(x):
    return x * jax.nn.sigmoid(x)


def _inproj_kernel(x_ref, mixw_ref, wrow_ref, wt_ref, nw_ref, rope_ref, wup_ref, bup_ref,
                   qT_ref, k_ref, vT_ref, ag_ref, lq_ref, lv_ref, lg_ref, lkT_ref, gT_ref):
    subs = [slice(j * TOKEN_SUB, (j + 1) * TOKEN_SUB) for j in range(x_ref.shape[0] // TOKEN_SUB)]
    xns, zts = [], []
    for ts in subs:
        x = x_ref[ts, :]
        xn = (x * lax.rsqrt(jnp.mean(x * x, axis=-1, keepdims=True) + EPS) * mixw_ref[...]).astype(BF16)
        xns.append(xn)
        zts.append(lax.dot_general(wt_ref[...], xn, NT_DIMS, preferred_element_type=F32))
    xn = jnp.concatenate(xns, axis=0)

    def norm_rope(zh, w, ts):
        cr, sr, cc, sc = rope_ref[0, :, ts], rope_ref[1, :, ts], rope_ref[2, :, ts], rope_ref[3, :, ts]
        y = zh * lax.rsqrt(jnp.mean(zh * zh, axis=0, keepdims=True) + EPS) * w
        r1, r2, c1, c2 = y[0:16], y[16:32], y[32:48], y[48:64]
        return jnp.concatenate([r1 * cr - r2 * sr, r2 * cr + r1 * sr,
                                c1 * cc - c2 * sc, c2 * cc + c1 * sc], axis=0)

    def q_epilogue(ts, zt):
        for h in range(N_Q_HEADS):
            qh = norm_rope(zt[h * HEAD_DIM:(h + 1) * HEAD_DIM], nw_ref[h], ts)
            qT_ref[0, h * HEAD_DIM:(h + 1) * HEAD_DIM, ts] = qh.astype(BF16)

    def kv_gla_epilogue(ts, zt):
        kT = jnp.concatenate(
            [norm_rope(zt[ATTN_WIDTH + h * HEAD_DIM:ATTN_WIDTH + (h + 1) * HEAD_DIM], nw_ref[N_Q_HEADS + h], ts)
             for h in range(N_KV_HEADS)], axis=0)
        k_ref[ts, :] = kT.T.astype(BF16)
        ones = jnp.ones((BF16_SUBLANES, TOKEN_SUB), F32)
        v0 = ATTN_WIDTH + KV_WIDTH
        vT = jnp.concatenate([zt[v0:v0 + HEAD_DIM], ones, zt[v0 + HEAD_DIM:v0 + 2 * HEAD_DIM], ones], axis=0)
        vT_ref[0, 0, :, ts] = vT.astype(BF16)
        l0 = v0 + KV_WIDTH
        lkT_ref[0, :, ts] = zt[l0:l0 + GLA_KEY_WIDTH]
        lrT = zt[l0 + GLA_KEY_WIDTH:l0 + GLA_KEY_WIDTH + 2 * GATE_RANK].astype(BF16)
        gpre = _dot(wup_ref[...], lrT) + bup_ref[...]
        gT_ref[0, :, ts] = jax.nn.log_sigmoid(gpre) * (1.0 / GATE_NORMALIZER)

    feature_major_epilogues = [functools.partial(f, ts, zt) for ts, zt in zip(subs, zts)
                               for f in (q_epilogue, kv_gla_epilogue)]
    groups = [(0, 512, ag_ref, _silu), (512, 768, lq_ref, None), (768, 1280, lv_ref, None),
              (1280, 1792, lg_ref, _silu)]
    token_major = []

    def store(gi):
        _, _, out_ref, act = groups[gi]
        out_ref[...] = token_major[gi] if act is None else act(token_major[gi])

    for gi, (c0, c1, _, _) in enumerate(groups):
        token_major.append(_dot(xn, wrow_ref[:, c0:c1]))
        if gi > 0:
            store(gi - 1)
        for epilogue in feature_major_epilogues[gi::len(groups)]:
            epilogue()
    store(len(groups) - 1)


def _inproj(x2d, B, T, tm, kc, mixw, wrow, wt, nw, rope, wup, bup):
    nt = T // tm
    per_kc = kc // tm
    grid = (B, nt)
    tok = lambda b, i: (b * nt + i, 0)
    fm = lambda b, i: (b, 0, i)
    const2 = lambda b, i: (0, 0)
    const3 = lambda b, i: (0, 0, 0)
    out_shape = (
        jax.ShapeDtypeStruct((B, ATTN_WIDTH, T), BF16),
        jax.ShapeDtypeStruct((B * T, KV_WIDTH), BF16),
        jax.ShapeDtypeStruct((B, T // kc, N_KV_HEADS * V_ROWS, kc), BF16),
        jax.ShapeDtypeStruct((B * T, ATTN_WIDTH), F32),
        jax.ShapeDtypeStruct((B * T, GLA_KEY_WIDTH), F32),
        jax.ShapeDtypeStruct((B * T, GLA_WIDTH), F32),
        jax.ShapeDtypeStruct((B * T, GLA_WIDTH), F32),
        jax.ShapeDtypeStruct((B, GLA_KEY_WIDTH, T), F32),
        jax.ShapeDtypeStruct((B, 2 * GLA_KEY_WIDTH, T), F32),
    )
    out_specs = (
        pl.BlockSpec((1, ATTN_WIDTH, tm), fm),
        pl.BlockSpec((tm, KV_WIDTH), tok),
        pl.BlockSpec((1, 1, N_KV_HEADS * V_ROWS, tm), lambda b, i: (b, i // per_kc, 0, i % per_kc)),
        pl.BlockSpec((tm, ATTN_WIDTH), tok),
        pl.BlockSpec((tm, GLA_KEY_WIDTH), tok),
        pl.BlockSpec((tm, GLA_WIDTH), tok),
        pl.BlockSpec((tm, GLA_WIDTH), tok),
        pl.BlockSpec((1, GLA_KEY_WIDTH, tm), fm),
        pl.BlockSpec((1, 2 * GLA_KEY_WIDTH, tm), fm),
    )
    in_specs = [
        pl.BlockSpec((tm, D_MODEL), tok),
        pl.BlockSpec(mixw.shape, const2),
        pl.BlockSpec(wrow.shape, const2),
        pl.BlockSpec(wt.shape, const2),
        pl.BlockSpec(nw.shape, const3),
        pl.BlockSpec((4, HEAD_DIM // 4, tm), lambda b, i: (0, 0, i)),
        pl.BlockSpec(wup.shape, const2),
        pl.BlockSpec(bup.shape, const2),
    ]
    return pl.pallas_call(
        _inproj_kernel, grid=grid, in_specs=in_specs, out_specs=out_specs, out_shape=out_shape,
        name="inproj",
        compiler_params=pltpu.CompilerParams(
            dimension_semantics=("arbitrary", "arbitrary"), vmem_limit_bytes=VMEM_LIMIT_BYTES),
    )(x2d, mixw, wrow, wt, nw, rope, wup, bup)


def _attn_kernel(qT_ref, k_ref, vT_ref, gate_ref, kb_ref, o_ref, w_ref, m_ref, acc_ref, sa_ref, sb_ref,
                 *, tq, kc, nchunks, ntiles):
    assert nchunks % 2 == 0
    g = pl.program_id(1)
    i = pl.program_id(2)
    nq = Q_PER_KV * tq
    cur = i % 2
    prv = 1 - cur
    last = nchunks - 1

    @pl.when(i < ntiles)
    def _():
        qcat = jnp.concatenate([qT_ref[0, h * HEAD_DIM:(h + 1) * HEAD_DIM, :] for h in range(Q_PER_KV)], axis=1)
        zeros = jnp.zeros_like(qcat)
        w_ref[cur] = jnp.where(g == 0, jnp.concatenate([qcat, zeros], axis=0),
                               jnp.concatenate([zeros, qcat], axis=0))
        qf = qcat.astype(F32)
        m_ref[cur] = jnp.sqrt(jnp.sum(qf * qf, axis=0, keepdims=True)) * kb_ref[...] - SHIFT_HEADROOM
        acc_ref[cur] = jnp.zeros((V_ROWS, nq), F32)

    def key_chunk(c):
        start = c * kc if isinstance(c, int) else pl.multiple_of(c * kc, kc)
        return k_ref[pl.ds(start, kc), :]

    def step(c_cur, slot_cur, s_cur, c_nxt, slot_nxt, s_nxt):
        kch = None if s_nxt is None else key_chunk(c_nxt)
        vch = None if s_cur is None else vT_ref[0, c_cur]
        for j in range(nq // Q_COLS):
            cols = slice(j * Q_COLS, (j + 1) * Q_COLS)
            for r in range(kc // KEY_ROWS):
                rows = slice(r * KEY_ROWS, (r + 1) * KEY_ROWS)
                if s_nxt is not None:
                    s_nxt[rows, cols] = _dot(kch[rows], w_ref[slot_nxt, :, cols])
                if s_cur is not None:
                    p = jnp.exp2(s_cur[rows, cols] - m_ref[slot_cur, :, cols]).astype(BF16)
                    acc_ref[slot_cur, :, cols] += _dot(vch[:, rows], p)

    @pl.when(i == 0)
    def _():
        step(None, None, None, 0, cur, sa_ref)

    @pl.when(jnp.logical_and(i > 0, i < ntiles))
    def _():
        step(last, prv, sb_ref, 0, cur, sa_ref)

    @pl.when(i == ntiles)
    def _():
        step(last, prv, sb_ref, None, None, None)

    @pl.when(i > 0)
    def _():
        @pl.when(jnp.logical_not(jnp.min(acc_ref[prv, HEAD_DIM:HEAD_DIM + 1, :]) >= MIN_DENOMINATOR))
        def _():
            m_ref[prv] = jnp.full((1, nq), NEG_BIG, F32)
            acc_ref[prv] = jnp.zeros((V_ROWS, nq), F32)

            def running_max_body(c, carry):
                s = _dot(key_chunk(c), w_ref[prv])
                m_old = m_ref[prv]
                m_new = jnp.maximum(m_old, jnp.max(s, axis=0, keepdims=True))
                p = jnp.exp2(s - m_new).astype(BF16)
                acc_ref[prv] = acc_ref[prv] * jnp.exp2(m_old - m_new) + _dot(vT_ref[0, c], p)
                m_ref[prv] = m_new
                return carry

            lax.fori_loop(0, nchunks, running_max_body, 0)

        acc = acc_ref[prv]
        o = acc[0:HEAD_DIM] / acc[HEAD_DIM:HEAD_DIM + 1]
        oT = jnp.concatenate([o[:, h * tq:(h + 1) * tq] for h in range(Q_PER_KV)], axis=0)
        o_ref[...] = (oT.T * gate_ref[...]).astype(BF16)

    @pl.when(i < ntiles)
    def _():
        def two_steps(it, carry):
            c = 2 * it
            step(c, cur, sa_ref, c + 1, cur, sb_ref)
            step(c + 1, cur, sb_ref, c + 2, cur, sa_ref)
            return carry

        lax.fori_loop(0, nchunks // 2 - 1, two_steps, 0)
        step(nchunks - 2, cur, sa_ref, last, cur, sb_ref)


def _attention(qT, k, vT, gate, kbound, B, T, tq, kc):
    ntiles = T // tq
    nchunks = T // kc
    gw = Q_PER_KV * HEAD_DIM
    nq = Q_PER_KV * tq
    kern = functools.partial(_attn_kernel, tq=tq, kc=kc, nchunks=nchunks, ntiles=ntiles)
    done = lambda b, g, i: (b * ntiles + jnp.maximum(i - 1, 0), g)
    return pl.pallas_call(
        kern,
        grid=(B, N_KV_HEADS, ntiles + 1),
        in_specs=[
            pl.BlockSpec((1, gw, tq), lambda b, g, i: (b, g, jnp.minimum(i, ntiles - 1))),
            pl.BlockSpec((T, KV_WIDTH), lambda b, g, i: (b, 0)),
            pl.BlockSpec((1, nchunks, V_ROWS, kc), lambda b, g, i: (b, 0, g, 0)),
            pl.BlockSpec((tq, gw), done),
            pl.BlockSpec((1, 1), lambda b, g, i: (0, 0)),
        ],
        out_specs=pl.BlockSpec((tq, gw), done),
        out_shape=jax.ShapeDtypeStruct((B * T, ATTN_WIDTH), BF16),
        scratch_shapes=[pltpu.VMEM((2, KV_WIDTH, nq), BF16), pltpu.VMEM((2, 1, nq), F32),
                        pltpu.VMEM((2, V_ROWS, nq), F32), pltpu.VMEM((kc, nq), F32), pltpu.VMEM((kc, nq), F32)],
        name="attention",
        compiler_params=pltpu.CompilerParams(
            dimension_semantics=("arbitrary", "arbitrary", "arbitrary"), vmem_limit_bytes=VMEM_LIMIT_BYTES),
    )(qT, k, vT, gate, kbound)


def _gla_chains(chains, lg):
    nch = lg // CHUNK
    t_src = lax.broadcasted_iota(jnp.int32, (lg, lg), 0)
    t_dst = lax.broadcasted_iota(jnp.int32, (lg, lg), 1)
    same = (t_src // CHUNK) == (t_dst // CHUNK)
    lane = lax.broadcasted_iota(jnp.int32, (GLA_KEY_WIDTH, lg), 1)
    heads = [(slice(h * GLA_DK, (h + 1) * GLA_DK), slice(h * GLA_DV, (h + 1) * GLA_DV)) for h in range(GLA_HEADS)]

    bTs = []
    for (_, _, _, gT, _, reverse) in chains:
        tri = jnp.where(same & ((t_src >= t_dst) if reverse else (t_src <= t_dst)), 1.0, 0.0).astype(BF16)
        g1 = gT.astype(BF16)
        r1 = gT - g1.astype(F32)
        g2 = r1.astype(BF16)
        g3 = (r1 - g2.astype(F32)).astype(BF16)
        bTs.append(_dot(g1, tri) + _dot(g2, tri) + _dot(g3, tri))

    prep = []
    for (lq, lv, kT, _, _, reverse), bT in zip(chains, bTs):
        tot_cols = []
        btot = jnp.zeros_like(bT)
        for n in range(nch):
            edge = n * CHUNK if reverse else n * CHUNK + CHUNK - 1
            col = bT[:, edge:edge + 1]
            tot_cols.append(jnp.exp(col))
            btot = jnp.where((lane // CHUNK) == n, col, btot)
        k_decT = (kT * jnp.exp(-bT)).astype(BF16)
        k_kvT = (kT * jnp.exp(btot - bT)).astype(BF16)
        q_dec = (lq * (GLA_DK ** -0.5) * jnp.exp(bT.T)).astype(BF16)
        v = lv.astype(BF16)
        intra = same & ((t_src < t_dst) if reverse else (t_src >= t_dst))
        prep.append((q_dec, k_decT, k_kvT, v, tot_cols, intra))

    a = [[jnp.where(intra, _dot(q_dec[:, ks], k_decT[ks, :]), 0.0).astype(BF16) for ks, _ in heads]
         for (q_dec, k_decT, _, _, _, intra) in prep]
    kv = [[[_dot(k_kvT[ks, n * CHUNK:(n + 1) * CHUNK], v[n * CHUNK:(n + 1) * CHUNK, vs]) for n in range(nch)]
           for ks, vs in heads] for (_, _, k_kvT, v, _, _) in prep]
    o = [[_dot(a[ci][h], v[:, vs]) for h, (_, vs) in enumerate(heads)]
         for ci, (_, _, _, v, _, _) in enumerate(prep)]

    outs = []
    for ci, ((_, _, _, _, s_ref, reverse), (q_dec, _, _, _, tot_cols, _)) in enumerate(zip(chains, prep)):
        order = range(nch - 1, -1, -1) if reverse else range(nch)
        per_head = []
        for h, (ks, _) in enumerate(heads):
            inter = [None] * nch
            state = s_ref[h]
            for n in order:
                inter[n] = _dot(q_dec[n * CHUNK:(n + 1) * CHUNK, ks], state.astype(BF16))
                state = tot_cols[n][ks] * state + kv[ci][h][n]
            s_ref[h] = state
            per_head.append(o[ci][h] + jnp.concatenate(inter, axis=0))
        outs.append(jnp.concatenate(per_head, axis=1))
    return outs


def _gla_kernel(lq_f, lv_f, lkT_f, gT_f, lq_b, lv_b, lkT_b, gT_b, of_ref, ob_ref, sf_ref, sb_ref, *, lg, nsub):
    @pl.when(pl.program_id(1) == 0)
    def _():
        sf_ref[...] = jnp.zeros_like(sf_ref)
        sb_ref[...] = jnp.zeros_like(sb_ref)

    def sub(refs, j, s_ref, reverse):
        lq_ref, lv_ref, lkT_ref, gT_ref = refs
        ts = slice(j * lg, (j + 1) * lg)
        return (lq_ref[ts, :], lv_ref[ts, :], lkT_ref[0, :, ts], gT_ref[0, :, ts], s_ref, reverse)

    fwd = [sub((lq_f, lv_f, lkT_f, gT_f), j, sf_ref, False) for j in range(nsub)]
    bwd = [sub((lq_b, lv_b, lkT_b, gT_b), j, sb_ref, True) for j in range(nsub - 1, -1, -1)]
    outs = _gla_chains(fwd + bwd, lg)
    for j in range(nsub):
        of_ref[j * lg:(j + 1) * lg, :] = outs[j]
        ob_ref[(nsub - 1 - j) * lg:(nsub - j) * lg, :] = outs[nsub + j]


def _gla(lq, lv, lkT, gT, B, T, lg, nsub):
    blk = lg * nsub
    nb = T // blk
    ftok = lambda b, i: (b * nb + i, 0)
    rtok = lambda b, i: (b * nb + (nb - 1 - i), 0)
    state = pltpu.VMEM((GLA_HEADS, GLA_DK, GLA_DV), F32)
    o_shape = jax.ShapeDtypeStruct((B * T, GLA_WIDTH), F32)
    return pl.pallas_call(
        functools.partial(_gla_kernel, lg=lg, nsub=nsub),
        grid=(B, nb),
        in_specs=[
            pl.BlockSpec((blk, GLA_KEY_WIDTH), ftok),
            pl.BlockSpec((blk, GLA_WIDTH), ftok),
            pl.BlockSpec((1, GLA_KEY_WIDTH, blk), lambda b, i: (b, 0, i)),
            pl.BlockSpec((1, GLA_KEY_WIDTH, blk), lambda b, i: (b, 0, i)),
            pl.BlockSpec((blk, GLA_KEY_WIDTH), rtok),
            pl.BlockSpec((blk, GLA_WIDTH), rtok),
            pl.BlockSpec((1, GLA_KEY_WIDTH, blk), lambda b, i: (b, 0, nb - 1 - i)),
            pl.BlockSpec((1, GLA_KEY_WIDTH, blk), lambda b, i: (b, 1, nb - 1 - i)),
        ],
        out_specs=(pl.BlockSpec((blk, GLA_WIDTH), ftok), pl.BlockSpec((blk, GLA_WIDTH), rtok)),
        out_shape=(o_shape, o_shape),
        scratch_shapes=[state, state], name="gla",
        compiler_params=pltpu.CompilerParams(dimension_semantics=("arbitrary", "arbitrary"),
                                             vmem_limit_bytes=VMEM_LIMIT_BYTES),
    )(lq, lv, lkT, gT, lq, lv, lkT, gT)


def _out_kernel(x_ref, ma_ref, of_ref, ob_ref, lgate_ref, glaw_ref, p_ref, wo_ref, plew_ref, wg_ref, wp_ref,
                fw_ref, y_ref, *, final):
    subs = [slice(j * TOKEN_SUB, (j + 1) * TOKEN_SUB) for j in range(x_ref.shape[0] // TOKEN_SUB)]
    ha = [x_ref[ts, :] + _dot(ma_ref[ts, :], wo_ref[0:ATTN_WIDTH, :]) for ts in subs]
    pe = [_dot(p_ref[ts, :].astype(BF16), wp_ref[...]) for ts in subs]
    hs = []
    for ts, h in zip(subs, ha):
        o = of_ref[ts, :] + ob_ref[ts, :]
        lgate = lgate_ref[ts, :]
        mg = []
        for hd in range(GLA_HEADS):
            vs = slice(hd * GLA_DV, (hd + 1) * GLA_DV)
            oh = o[:, vs]
            y = oh * lax.rsqrt(jnp.mean(oh * oh, axis=-1, keepdims=True) + EPS) * glaw_ref[...]
            mg.append((y * lgate[:, vs]).astype(BF16))
        hs.append(h + _dot(jnp.concatenate(mg, axis=1), wo_ref[ATTN_WIDTH:, :]))
    gates = []
    for h in hs:
        hn = h * lax.rsqrt(jnp.mean(h * h, axis=-1, keepdims=True) + EPS) * plew_ref[...]
        gates.append(_dot(hn.astype(BF16), wg_ref[...]))
    for ts, h, g, e in zip(subs, hs, gates, pe):
        h = h + jax.nn.sigmoid(g) * e
        if final:
            h = h * lax.rsqrt(jnp.mean(h * h, axis=-1, keepdims=True) + EPS) * fw_ref[...]
        y_ref[ts, :] = h


def _outproj(x2d, ma, o_fwd, o_bwd, lgate, glaw, p2d, wo, plew, wg, wp, fw, tm, final):
    n = x2d.shape[0]
    tok = lambda i: (i, 0)
    const = lambda i: (0, 0)
    return pl.pallas_call(
        functools.partial(_out_kernel, final=final),
        grid=(n // tm,),
        in_specs=[
            pl.BlockSpec((tm, D_MODEL), tok),
            pl.BlockSpec((tm, ATTN_WIDTH), tok),
            pl.BlockSpec((tm, GLA_WIDTH), tok),
            pl.BlockSpec((tm, GLA_WIDTH), tok),
            pl.BlockSpec((tm, GLA_WIDTH), tok),
            pl.BlockSpec(glaw.shape, const),
            pl.BlockSpec((tm, PLE_DIM), tok),
            pl.BlockSpec(wo.shape, const),
            pl.BlockSpec(plew.shape, const),
            pl.BlockSpec(wg.shape, const),
            pl.BlockSpec(wp.shape, const),
            pl.BlockSpec(fw.shape, const),
        ],
        out_specs=pl.BlockSpec((tm, D_MODEL), tok),
        out_shape=jax.ShapeDtypeStruct((n, D_MODEL), F32),
        name="outproj",
        compiler_params=pltpu.CompilerParams(
            dimension_semantics=("arbitrary",), vmem_limit_bytes=VMEM_LIMIT_BYTES),
    )(x2d, ma, o_fwd, o_bwd, lgate, glaw, p2d, wo, plew, wg, wp, fw)


def _rope_tables(T):
    rows = T // GRID_W
    half = HEAD_DIM // 2
    inv_freq = ROPE_THETA ** (-jnp.arange(0, half, 2, dtype=F32) / half)
    ang_r = inv_freq[:, None] * jnp.arange(rows, dtype=F32)[None, :]
    ang_c = inv_freq[:, None] * jnp.arange(GRID_W, dtype=F32)[None, :]
    by_row = lambda a: jnp.repeat(a, GRID_W, axis=1)
    by_col = lambda a: jnp.tile(a, (1, rows))
    return jnp.stack([by_row(jnp.cos(ang_r)), by_row(jnp.sin(ang_r)),
                      by_col(jnp.cos(ang_c)), by_col(jnp.sin(ang_c))])


def _prep_layer_weights(mix_norm, w_in, q_norm, k_norm, w_up_f, b_f, w_up_b, b_b, gla_norm, w_out,
                        ple_norm, w_ple_gate, w_ple_proj):
    c = [0, 512, 640, 768, 1280, 1536, 1792, 2304, 2816, 2832, 2848]
    seg = lambda j: w_in[:, c[j]:c[j + 1]]
    a_q, a_k, a_v, a_gate, l_q, l_k, l_v, l_gate, lr_f, lr_b = (seg(j) for j in range(10))
    wrow = jnp.concatenate([a_gate, l_q, l_v, l_gate], axis=1).astype(BF16)
    wt = jnp.concatenate([a_q, a_k, a_v, l_k, lr_f, lr_b], axis=1).T.astype(BF16)
    qs = q_norm * (HEAD_DIM ** -0.5 * LOG2E)
    nw = jnp.concatenate([jnp.tile(qs[None], (N_Q_HEADS, 1)), jnp.tile(k_norm[None], (N_KV_HEADS, 1))])[..., None]
    zero = jnp.zeros((GLA_KEY_WIDTH, GATE_RANK), F32)
    wup = jnp.concatenate([jnp.concatenate([w_up_f.T, zero], axis=1),
                           jnp.concatenate([zero, w_up_b.T], axis=1)], axis=0).astype(BF16)
    bup = jnp.concatenate([b_f, b_b])[:, None]
    kbound = (jnp.max(jnp.abs(k_norm)) * (HEAD_DIM ** 0.5 * (1.0 + 2.0 ** -7))).reshape(1, 1)
    return dict(mixw=mix_norm[None], wrow=wrow, wt=wt, nw=nw, wup=wup, bup=bup, kbound=kbound,
                gla_nw=gla_norm[None],
                wo=w_out.astype(BF16), plew=ple_norm[None], wg=w_ple_gate.astype(BF16),
                wp=w_ple_proj.astype(BF16))


def _layer(h2d, p2d, B, T, w, fw, final, tiles):
    tm, tq, kc, lg, tmo = tiles
    rope = _rope_tables(T)
    qT, k, vT, ag, lq, lv, lgate, lkT, gT = _inproj(
        h2d, B, T, tm, kc, w["mixw"], w["wrow"], w["wt"], w["nw"], rope, w["wup"], w["bup"])
    ma = _attention(qT, k, vT, ag, w["kbound"], B, T, tq, kc)
    o_fwd, o_bwd = _gla(lq, lv, lkT, gT, B, T, lg, GLA_SUB_BLOCKS)
    return _outproj(h2d, ma, o_fwd, o_bwd, lgate, w["gla_nw"], p2d, w["wo"], w["plew"], w["wg"], w["wp"], fw,
                    tmo, final)


def _pick_tiles(T):
    return (min(512, T), min(512, T), min(1024, T), min(256, T), min(512, T))


def kernel(x_prompt, x_sample, p_prompt, p_sample, mix_norm, w_in, q_norm, k_norm, w_gate_up_fwd,
           b_gate_fwd, w_gate_up_bwd, b_gate_bwd, gla_norm, w_out, ple_norm, w_ple_gate, w_ple_proj,
           final_norm):
    depth = w_in.shape[0]
    fw = final_norm[None]
    outs = []
    for x, p in ((x_prompt, p_prompt), (x_sample, p_sample)):
        B, T, _ = x.shape
        h = x.reshape(B * T, D_MODEL)
        for i in range(depth):
            w = _prep_layer_weights(mix_norm[i], w_in[i], q_norm[i], k_norm[i], w_gate_up_fwd[i],
                                    b_gate_fwd[i], w_gate_up_bwd[i], b_gate_bwd[i], gla_norm[i], w_out[i],
                                    ple_norm[i], w_ple_gate[i], w_ple_proj[i])
            h = _layer(h, p[i].reshape(B * T, PLE_DIM), B, T, w, fw, i == depth - 1, _pick_tiles(T))
        outs.append(h.reshape(B, T, D_MODEL))
    return tuple(outs)
```

```python
import functools
import math

import jax
import jax.numpy as jnp
from jax import lax
from jax.experimental import pallas as pl
from jax.experimental.pallas import tpu as pltpu

F32 = jnp.float32
BF16 = jnp.bfloat16

D_MODEL = 1024
GRID_W = 64
N_Q_HEADS = 8
N_KV_HEADS = 2
HEAD_DIM = 64
Q_PER_KV = N_Q_HEADS // N_KV_HEADS
ATTN_WIDTH = N_Q_HEADS * HEAD_DIM
KV_WIDTH = N_KV_HEADS * HEAD_DIM
ROPE_THETA = 10000.0
GLA_HEADS = 4
GLA_DK = 64
GLA_DV = 128
GLA_KEY_WIDTH = GLA_HEADS * GLA_DK
GLA_WIDTH = GLA_HEADS * GLA_DV
GATE_RANK = 16
GATE_NORMALIZER = 16.0
CHUNK = 64
PLE_DIM = 256
EPS = 1e-6
LOG2E = math.log2(math.e)

VMEM_LIMIT_BYTES = 56 * 1024 * 1024
BF16_SUBLANES = 16
V_ROWS = HEAD_DIM + BF16_SUBLANES
NEG_BIG = -1e30
Q_COLS = 512
KEY_ROWS = 256
LOOP_STEPS = 4
SHIFT_HEADROOM = 100.0
MIN_DENOMINATOR = 2.0 ** -60
TOKEN_SUB = 256
GLA_SUB_BLOCKS = 4

NT_DIMS = (((1,), (1,)), ((), ()))


def _dot(a, b):
    return jnp.dot(a, b, preferred_element_type=F32)


def _silu(x):
    return x * jax.nn.sigmoid(x)


def _compiler_params(grid_rank):
    return pltpu.CompilerParams(dimension_semantics=("arbitrary",) * grid_rank,
                                vmem_limit_bytes=VMEM_LIMIT_BYTES)


def _inproj_kernel(x_ref, mixw_ref, wrow_ref, wt_ref, nw_ref, rope_ref, wup_ref, bup_ref,
                   qT_ref, k_ref, vT_ref, ag_ref, lq_ref, lv_ref, lg_ref, lkT_ref, gT_ref):
    subs = [slice(j * TOKEN_SUB, (j + 1) * TOKEN_SUB) for j in range(x_ref.shape[0] // TOKEN_SUB)]
    xns, zts = [], []
    for ts in subs:
        x = x_ref[ts, :]
        xn = (x * lax.rsqrt(jnp.mean(x * x, axis=-1, keepdims=True) + EPS) * mixw_ref[...]).astype(BF16)
        xns.append(xn)
        zts.append(lax.dot_general(wt_ref[...], xn, NT_DIMS, preferred_element_type=F32))
    xn = jnp.concatenate(xns, axis=0)

    def norm_rope(zh, w, ts):
        cr, sr, cc, sc = rope_ref[0, :, ts], rope_ref[1, :, ts], rope_ref[2, :, ts], rope_ref[3, :, ts]
        y = zh * lax.rsqrt(jnp.mean(zh * zh, axis=0, keepdims=True) + EPS) * w
        r1, r2, c1, c2 = y[0:16], y[16:32], y[32:48], y[48:64]
        return jnp.concatenate([r1 * cr - r2 * sr, r2 * cr + r1 * sr,
                                c1 * cc - c2 * sc, c2 * cc + c1 * sc], axis=0)

    def q_epilogue(ts, zt):
        for h in range(N_Q_HEADS):
            qh = norm_rope(zt[h * HEAD_DIM:(h + 1) * HEAD_DIM], nw_ref[h], ts)
            qT_ref[0, h * HEAD_DIM:(h + 1) * HEAD_DIM, ts] = qh.astype(BF16)

    def kv_gla_epilogue(ts, zt):
        kT = jnp.concatenate(
            [norm_rope(zt[ATTN_WIDTH + h * HEAD_DIM:ATTN_WIDTH + (h + 1) * HEAD_DIM], nw_ref[N_Q_HEADS + h], ts)
             for h in range(N_KV_HEADS)], axis=0)
        k_ref[ts, :] = kT.T.astype(BF16)
        ones = jnp.ones((BF16_SUBLANES, TOKEN_SUB), F32)
        v0 = ATTN_WIDTH + KV_WIDTH
        vT = jnp.concatenate([zt[v0:v0 + HEAD_DIM], ones, zt[v0 + HEAD_DIM:v0 + 2 * HEAD_DIM], ones], axis=0)
        vT_ref[0, 0, :, ts] = vT.astype(BF16)
        l0 = v0 + KV_WIDTH
        lkT_ref[0, :, ts] = zt[l0:l0 + GLA_KEY_WIDTH]
        lrT = zt[l0 + GLA_KEY_WIDTH:l0 + GLA_KEY_WIDTH + 2 * GATE_RANK].astype(BF16)
        gpre = _dot(wup_ref[...], lrT) + bup_ref[...]
        gT_ref[0, :, ts] = jax.nn.log_sigmoid(gpre) * (1.0 / GATE_NORMALIZER)

    feature_major_epilogues = [functools.partial(f, ts, zt) for ts, zt in zip(subs, zts)
                               for f in (q_epilogue, kv_gla_epilogue)]
    groups = [(0, 512, ag_ref, _silu), (512, 768, lq_ref, None), (768, 1280, lv_ref, None),
              (1280, 1792, lg_ref, _silu)]
    token_major = []

    def store(gi):
        _, _, out_ref, act = groups[gi]
        out_ref[...] = token_major[gi] if act is None else act(token_major[gi])

    for gi, (c0, c1, _, _) in enumerate(groups):
        token_major.append(_dot(xn, wrow_ref[:, c0:c1]))
        if gi > 0:
            store(gi - 1)
        for epilogue in feature_major_epilogues[gi::len(groups)]:
            epilogue()
    store(len(groups) - 1)


def _inproj(x2d, B, T, tm, kc, mixw, wrow, wt, nw, rope, wup, bup):
    nt = T // tm
    per_kc = kc // tm
    grid = (B, nt)
    tok = lambda b, i: (b * nt + i, 0)
    fm = lambda b, i: (b, 0, i)
    const2 = lambda b, i: (0, 0)
    const3 = lambda b, i: (0, 0, 0)
    out_shape = (
        jax.ShapeDtypeStruct((B, ATTN_WIDTH, T), BF16),
        jax.ShapeDtypeStruct((B * T, KV_WIDTH), BF16),
        jax.ShapeDtypeStruct((B, T // kc, N_KV_HEADS * V_ROWS, kc), BF16),
        jax.ShapeDtypeStruct((B * T, ATTN_WIDTH), F32),
        jax.ShapeDtypeStruct((B * T, GLA_KEY_WIDTH), F32),
        jax.ShapeDtypeStruct((B * T, GLA_WIDTH), F32),
        jax.ShapeDtypeStruct((B * T, GLA_WIDTH), F32),
        jax.ShapeDtypeStruct((B, GLA_KEY_WIDTH, T), F32),
        jax.ShapeDtypeStruct((B, 2 * GLA_KEY_WIDTH, T), F32),
    )
    out_specs = (
        pl.BlockSpec((1, ATTN_WIDTH, tm), fm),
        pl.BlockSpec((tm, KV_WIDTH), tok),
        pl.BlockSpec((1, 1, N_KV_HEADS * V_ROWS, tm), lambda b, i: (b, i // per_kc, 0, i % per_kc)),
        pl.BlockSpec((tm, ATTN_WIDTH), tok),
        pl.BlockSpec((tm, GLA_KEY_WIDTH), tok),
        pl.BlockSpec((tm, GLA_WIDTH), tok),
        pl.BlockSpec((tm, GLA_WIDTH), tok),
        pl.BlockSpec((1, GLA_KEY_WIDTH, tm), fm),
        pl.BlockSpec((1, 2 * GLA_KEY_WIDTH, tm), fm),
    )
    in_specs = [
        pl.BlockSpec((tm, D_MODEL), tok),
        pl.BlockSpec(mixw.shape, const2),
        pl.BlockSpec(wrow.shape, const2),
        pl.BlockSpec(wt.shape, const2),
        pl.BlockSpec(nw.shape, const3),
        pl.BlockSpec((4, HEAD_DIM // 4, tm), lambda b, i: (0, 0, i)),
        pl.BlockSpec(wup.shape, const2),
        pl.BlockSpec(bup.shape, const2),
    ]
    return pl.pallas_call(
        _inproj_kernel, grid=grid, in_specs=in_specs, out_specs=out_specs, out_shape=out_shape,
        name="inproj",
        compiler_params=_compiler_params(2),
    )(x2d, mixw, wrow, wt, nw, rope, wup, bup)


def _attn_kernel(qT_ref, k_ref, vT_ref, gate_ref, kb_ref, o_ref, w_ref, m_ref, acc_ref, sa_ref, sb_ref,
                 *, tq, kc, nchunks, ntiles):
    assert nchunks % 2 == 0
    g = pl.program_id(1)
    i = pl.program_id(2)
    nq = Q_PER_KV * tq
    cur = i % 2
    prv = 1 - cur
    last = nchunks - 1

    @pl.when(i < ntiles)
    def _():
        qcat = jnp.concatenate([qT_ref[0, h * HEAD_DIM:(h + 1) * HEAD_DIM, :] for h in range(Q_PER_KV)], axis=1)
        zeros = jnp.zeros_like(qcat)
        w_ref[cur] = jnp.where(g == 0, jnp.concatenate([qcat, zeros], axis=0),
                               jnp.concatenate([zeros, qcat], axis=0))
        qf = qcat.astype(F32)
        m_ref[cur] = jnp.sqrt(jnp.sum(qf * qf, axis=0, keepdims=True)) * kb_ref[...] - SHIFT_HEADROOM
        acc_ref[cur] = jnp.zeros((V_ROWS, nq), F32)

    def key_chunk(c):
        start = c * kc if isinstance(c, int) else pl.multiple_of(c * kc, kc)
        return k_ref[pl.ds(start, kc), :]

    def step(c_cur, slot_cur, s_cur, c_nxt, slot_nxt, s_nxt):
        kch = None if s_nxt is None else key_chunk(c_nxt)
        vch = None if s_cur is None else vT_ref[0, c_cur]
        for j in range(nq // Q_COLS):
            cols = slice(j * Q_COLS, (j + 1) * Q_COLS)
            for r in range(kc // KEY_ROWS):
                rows = slice(r * KEY_ROWS, (r + 1) * KEY_ROWS)
                if s_nxt is not None:
                    s_nxt[rows, cols] = _dot(kch[rows], w_ref[slot_nxt, :, cols])
                if s_cur is not None:
                    p = jnp.exp2(s_cur[rows, cols] - m_ref[slot_cur, :, cols]).astype(BF16)
                    acc_ref[slot_cur, :, cols] += _dot(vch[:, rows], p)

    @pl.when(i == 0)
    def _():
        step(None, None, None, 0, cur, sa_ref)

    @pl.when(jnp.logical_and(i > 0, i < ntiles))
    def _():
        step(last, prv, sb_ref, 0, cur, sa_ref)

    @pl.when(i == ntiles)
    def _():
        step(last, prv, sb_ref, None, None, None)

    @pl.when(i > 0)
    def _():
        @pl.when(jnp.logical_not(jnp.min(acc_ref[prv, HEAD_DIM:HEAD_DIM + 1, :]) >= MIN_DENOMINATOR))
        def _():
            m_ref[prv] = jnp.full((1, nq), NEG_BIG, F32)
            acc_ref[prv] = jnp.zeros((V_ROWS, nq), F32)

            def running_max_body(c, carry):
                s = _dot(key_chunk(c), w_ref[prv])
                m_old = m_ref[prv]
                m_new = jnp.maximum(m_old, jnp.max(s, axis=0, keepdims=True))
                p = jnp.exp2(s - m_new).astype(BF16)
                acc_ref[prv] = acc_ref[prv] * jnp.exp2(m_old - m_new) + _dot(vT_ref[0, c], p)
                m_ref[prv] = m_new
                return carry

            lax.fori_loop(0, nchunks, running_max_body, 0)

        acc = acc_ref[prv]
        o = acc[0:HEAD_DIM] / acc[HEAD_DIM:HEAD_DIM + 1]
        oT = jnp.concatenate([o[:, h * tq:(h + 1) * tq] for h in range(Q_PER_KV)], axis=0)
        o_ref[...] = (oT.T * gate_ref[...]).astype(BF16)

    @pl.when(i < ntiles)
    def _():
        def steps(c0, count):
            for d in range(count):
                bufs = (sa_ref, sb_ref) if d % 2 == 0 else (sb_ref, sa_ref)
                step(c0 + d, cur, bufs[0], c0 + d + 1, cur, bufs[1])

        def loop_body(it, carry):
            steps(LOOP_STEPS * it, LOOP_STEPS)
            return carry

        trips = (nchunks - 1) // LOOP_STEPS
        if trips > 0:
            lax.fori_loop(0, trips, loop_body, 0)
        steps(trips * LOOP_STEPS, nchunks - 1 - trips * LOOP_STEPS)


def _attention(qT, k, vT, gate, kbound, B, T, tq, kc):
    ntiles = T // tq
    nchunks = T // kc
    gw = Q_PER_KV * HEAD_DIM
    nq = Q_PER_KV * tq
    kern = functools.partial(_attn_kernel, tq=tq, kc=kc, nchunks=nchunks, ntiles=ntiles)
    done = lambda b, g, i: (b * ntiles + jnp.maximum(i - 1, 0), g)
    return pl.pallas_call(
        kern,
        grid=(B, N_KV_HEADS, ntiles + 1),
        in_specs=[
            pl.BlockSpec((1, gw, tq), lambda b, g, i: (b, g, jnp.minimum(i, ntiles - 1))),
            pl.BlockSpec((T, KV_WIDTH), lambda b, g, i: (b, 0)),
            pl.BlockSpec((1, nchunks, V_ROWS, kc), lambda b, g, i: (b, 0, g, 0)),
            pl.BlockSpec((tq, gw), done),
            pl.BlockSpec((1, 1), lambda b, g, i: (0, 0)),
        ],
        out_specs=pl.BlockSpec((tq, gw), done),
        out_shape=jax.ShapeDtypeStruct((B * T, ATTN_WIDTH), BF16),
        scratch_shapes=[pltpu.VMEM((2, KV_WIDTH, nq), BF16), pltpu.VMEM((2, 1, nq), F32),
                        pltpu.VMEM((2, V_ROWS, nq), F32), pltpu.VMEM((kc, nq), F32), pltpu.VMEM((kc, nq), F32)],
        name="attention",
        compiler_params=_compiler_params(3),
    )(qT, k, vT, gate, kbound)


def _gla_chains(chains, lg):
    nch = lg // CHUNK
    t_src = lax.broadcasted_iota(jnp.int32, (lg, lg), 0)
    t_dst = lax.broadcasted_iota(jnp.int32, (lg, lg), 1)
    same = (t_src // CHUNK) == (t_dst // CHUNK)
    lane = lax.broadcasted_iota(jnp.int32, (GLA_KEY_WIDTH, lg), 1)
    heads = [(slice(h * GLA_DK, (h + 1) * GLA_DK), slice(h * GLA_DV, (h + 1) * GLA_DV)) for h in range(GLA_HEADS)]

    bTs = []
    for (_, _, _, gT, _, reverse) in chains:
        tri = jnp.where(same & ((t_src >= t_dst) if reverse else (t_src <= t_dst)), 1.0, 0.0).astype(BF16)
        g1 = gT.astype(BF16)
        r1 = gT - g1.astype(F32)
        g2 = r1.astype(BF16)
        g3 = (r1 - g2.astype(F32)).astype(BF16)
        bTs.append(_dot(g1, tri) + _dot(g2, tri) + _dot(g3, tri))

    prep = []
    for (lq, lv, kT, _, _, reverse), bT in zip(chains, bTs):
        tot_cols = []
        btot = jnp.zeros_like(bT)
        for n in range(nch):
            edge = n * CHUNK if reverse else n * CHUNK + CHUNK - 1
            col = bT[:, edge:edge + 1]
            tot_cols.append(jnp.exp(col))
            btot = jnp.where((lane // CHUNK) == n, col, btot)
        k_decT = (kT * jnp.exp(-bT)).astype(BF16)
        k_kvT = (kT * jnp.exp(btot - bT)).astype(BF16)
        q_dec = (lq * (GLA_DK ** -0.5) * jnp.exp(bT.T)).astype(BF16)
        v = lv.astype(BF16)
        intra = same & ((t_src < t_dst) if reverse else (t_src >= t_dst))
        prep.append((q_dec, k_decT, k_kvT, v, tot_cols, intra))

    a = [[jnp.where(intra, _dot(q_dec[:, ks], k_decT[ks, :]), 0.0).astype(BF16) for ks, _ in heads]
         for (q_dec, k_decT, _, _, _, intra) in prep]
    kv = [[[_dot(k_kvT[ks, n * CHUNK:(n + 1) * CHUNK], v[n * CHUNK:(n + 1) * CHUNK, vs]) for n in range(nch)]
           for ks, vs in heads] for (_, _, k_kvT, v, _, _) in prep]
    o = [[_dot(a[ci][h], v[:, vs]) for h, (_, vs) in enumerate(heads)]
         for ci, (_, _, _, v, _, _) in enumerate(prep)]

    outs = []
    for ci, ((_, _, _, _, s_ref, reverse), (q_dec, _, _, _, tot_cols, _)) in enumerate(zip(chains, prep)):
        order = range(nch - 1, -1, -1) if reverse else range(nch)
        per_head = []
        for h, (ks, _) in enumerate(heads):
            inter = [None] * nch
            state = s_ref[h]
            for n in order:
                inter[n] = _dot(q_dec[n * CHUNK:(n + 1) * CHUNK, ks], state.astype(BF16))
                state = tot_cols[n][ks] * state + kv[ci][h][n]
            s_ref[h] = state
            per_head.append(o[ci][h] + jnp.concatenate(inter, axis=0))
        outs.append(jnp.concatenate(per_head, axis=1))
    return outs


def _gla_kernel(lq_f, lv_f, lkT_f, gT_f, lq_b, lv_b, lkT_b, gT_b, of_ref, ob_ref, sf_ref, sb_ref, *, lg, nsub):
    @pl.when(pl.program_id(1) == 0)
    def _():
        sf_ref[...] = jnp.zeros_like(sf_ref)
        sb_ref[...] = jnp.zeros_like(sb_ref)

    def sub(refs, j, s_ref, reverse):
        lq_ref, lv_ref, lkT_ref, gT_ref = refs
        ts = slice(j * lg, (j + 1) * lg)
        return (lq_ref[ts, :], lv_ref[ts, :], lkT_ref[0, :, ts], gT_ref[0, :, ts], s_ref, reverse)

    fwd = [sub((lq_f, lv_f, lkT_f, gT_f), j, sf_ref, False) for j in range(nsub)]
    bwd = [sub((lq_b, lv_b, lkT_b, gT_b), j, sb_ref, True) for j in range(nsub - 1, -1, -1)]
    outs = _gla_chains(fwd + bwd, lg)
    for j in range(nsub):
        of_ref[j * lg:(j + 1) * lg, :] = outs[j]
        ob_ref[(nsub - 1 - j) * lg:(nsub - j) * lg, :] = outs[nsub + j]


def _gla(lq, lv, lkT, gT, B, T, lg, nsub):
    blk = lg * nsub
    nb = T // blk
    ftok = lambda b, i: (b * nb + i, 0)
    rtok = lambda b, i: (b * nb + (nb - 1 - i), 0)
    state = pltpu.VMEM((GLA_HEADS, GLA_DK, GLA_DV), F32)
    o_shape = jax.ShapeDtypeStruct((B * T, GLA_WIDTH), F32)
    return pl.pallas_call(
        functools.partial(_gla_kernel, lg=lg, nsub=nsub),
        grid=(B, nb),
        in_specs=[
            pl.BlockSpec((blk, GLA_KEY_WIDTH), ftok),
            pl.BlockSpec((blk, GLA_WIDTH), ftok),
            pl.BlockSpec((1, GLA_KEY_WIDTH, blk), lambda b, i: (b, 0, i)),
            pl.BlockSpec((1, GLA_KEY_WIDTH, blk), lambda b, i: (b, 0, i)),
            pl.BlockSpec((blk, GLA_KEY_WIDTH), rtok),
            pl.BlockSpec((blk, GLA_WIDTH), rtok),
            pl.BlockSpec((1, GLA_KEY_WIDTH, blk), lambda b, i: (b, 0, nb - 1 - i)),
            pl.BlockSpec((1, GLA_KEY_WIDTH, blk), lambda b, i: (b, 1, nb - 1 - i)),
        ],
        out_specs=(pl.BlockSpec((blk, GLA_WIDTH), ftok), pl.BlockSpec((blk, GLA_WIDTH), rtok)),
        out_shape=(o_shape, o_shape),
        scratch_shapes=[state, state], name="gla",
        compiler_params=_compiler_params(2),
    )(lq, lv, lkT, gT, lq, lv, lkT, gT)


def _out_kernel(x_ref, ma_ref, of_ref, ob_ref, lgate_ref, glaw_ref, p_ref, wo_ref, plew_ref, wg_ref, wp_ref,
                fw_ref, y_ref, *, final):
    subs = [slice(j * TOKEN_SUB, (j + 1) * TOKEN_SUB) for j in range(x_ref.shape[0] // TOKEN_SUB)]
    ha = [x_ref[ts, :] + _dot(ma_ref[ts, :], wo_ref[0:ATTN_WIDTH, :]) for ts in subs]
    pe = [_dot(p_ref[ts, :].astype(BF16), wp_ref[...]) for ts in subs]
    hs = []
    for ts, h in zip(subs, ha):
        o = of_ref[ts, :] + ob_ref[ts, :]
        lgate = lgate_ref[ts, :]
        mg = []
        for hd in range(GLA_HEADS):
            vs = slice(hd * GLA_DV, (hd + 1) * GLA_DV)
            oh = o[:, vs]
            y = oh * lax.rsqrt(jnp.mean(oh * oh, axis=-1, keepdims=True) + EPS) * glaw_ref[...]
            mg.append((y * lgate[:, vs]).astype(BF16))
        hs.append(h + _dot(jnp.concatenate(mg, axis=1), wo_ref[ATTN_WIDTH:, :]))
    gates = []
    for h in hs:
        hn = h * lax.rsqrt(jnp.mean(h * h, axis=-1, keepdims=True) + EPS) * plew_ref[...]
        gates.append(_dot(hn.astype(BF16), wg_ref[...]))
    for ts, h, g, e in zip(subs, hs, gates, pe):
        h = h + jax.nn.sigmoid(g) * e
        if final:
            h = h * lax.rsqrt(jnp.mean(h * h, axis=-1, keepdims=True) + EPS) * fw_ref[...]
        y_ref[ts, :] = h


def _outproj(x2d, ma, o_fwd, o_bwd, lgate, glaw, p2d, wo, plew, wg, wp, fw, tm, final):
    n = x2d.shape[0]
    tok = lambda i: (i, 0)
    const = lambda i: (0, 0)
    return pl.pallas_call(
        functools.partial(_out_kernel, final=final),
        grid=(n // tm,),
        in_specs=[
            pl.BlockSpec((tm, D_MODEL), tok),
            pl.BlockSpec((tm, ATTN_WIDTH), tok),
            pl.BlockSpec((tm, GLA_WIDTH), tok),
            pl.BlockSpec((tm, GLA_WIDTH), tok),
            pl.BlockSpec((tm, GLA_WIDTH), tok),
            pl.BlockSpec(glaw.shape, const),
            pl.BlockSpec((tm, PLE_DIM), tok),
            pl.BlockSpec(wo.shape, const),
            pl.BlockSpec(plew.shape, const),
            pl.BlockSpec(wg.shape, const),
            pl.BlockSpec(wp.shape, const),
            pl.BlockSpec(fw.shape, const),
        ],
        out_specs=pl.BlockSpec((tm, D_MODEL), tok),
        out_shape=jax.ShapeDtypeStruct((n, D_MODEL), F32),
        name="outproj",
        compiler_params=_compiler_params(1),
    )(x2d, ma, o_fwd, o_bwd, lgate, glaw, p2d, wo, plew, wg, wp, fw)


def _rope_tables(T):
    rows = T // GRID_W
    half = HEAD_DIM // 2
    inv_freq = ROPE_THETA ** (-jnp.arange(0, half, 2, dtype=F32) / half)
    ang_r = inv_freq[:, None] * jnp.arange(rows, dtype=F32)[None, :]
    ang_c = inv_freq[:, None] * jnp.arange(GRID_W, dtype=F32)[None, :]
    by_row = lambda a: jnp.repeat(a, GRID_W, axis=1)
    by_col = lambda a: jnp.tile(a, (1, rows))
    return jnp.stack([by_row(jnp.cos(ang_r)), by_row(jnp.sin(ang_r)),
                      by_col(jnp.cos(ang_c)), by_col(jnp.sin(ang_c))])


def _prep_layer_weights(mix_norm, w_in, q_norm, k_norm, w_up_f, b_f, w_up_b, b_b, gla_norm, w_out,
                        ple_norm, w_ple_gate, w_ple_proj):
    c = [0, 512, 640, 768, 1280, 1536, 1792, 2304, 2816, 2832, 2848]
    seg = lambda j: w_in[:, c[j]:c[j + 1]]
    a_q, a_k, a_v, a_gate, l_q, l_k, l_v, l_gate, lr_f, lr_b = (seg(j) for j in range(10))
    wrow = jnp.concatenate([a_gate, l_q, l_v, l_gate], axis=1).astype(BF16)
    wt = jnp.concatenate([a_q, a_k, a_v, l_k, lr_f, lr_b], axis=1).T.astype(BF16)
    qs = q_norm * (HEAD_DIM ** -0.5 * LOG2E)
    nw = jnp.concatenate([jnp.tile(qs[None], (N_Q_HEADS, 1)), jnp.tile(k_norm[None], (N_KV_HEADS, 1))])[..., None]
    zero = jnp.zeros((GLA_KEY_WIDTH, GATE_RANK), F32)
    wup = jnp.concatenate([jnp.concatenate([w_up_f.T, zero], axis=1),
                           jnp.concatenate([zero, w_up_b.T], axis=1)], axis=0).astype(BF16)
    bup = jnp.concatenate([b_f, b_b])[:, None]
    kbound = (jnp.max(jnp.abs(k_norm)) * (HEAD_DIM ** 0.5 * (1.0 + 2.0 ** -7))).reshape(1, 1)
    return dict(mixw=mix_norm[None], wrow=wrow, wt=wt, nw=nw, wup=wup, bup=bup, kbound=kbound,
                gla_nw=gla_norm[None],
                wo=w_out.astype(BF16), plew=ple_norm[None], wg=w_ple_gate.astype(BF16),
                wp=w_ple_proj.astype(BF16))


def _layer(h2d, p2d, B, T, w, fw, final, tiles):
    tm, tq, kc, lg, tmo = tiles
    rope = _rope_tables(T)
    qT, k, vT, ag, lq, lv, lgate, lkT, gT = _inproj(
        h2d, B, T, tm, kc, w["mixw"], w["wrow"], w["wt"], w["nw"], rope, w["wup"], w["bup"])
    ma = _attention(qT, k, vT, ag, w["kbound"], B, T, tq, kc)
    o_fwd, o_bwd = _gla(lq, lv, lkT, gT, B, T, lg, GLA_SUB_BLOCKS)
    return _outproj(h2d, ma, o_fwd, o_bwd, lgate, w["gla_nw"], p2d, w["wo"], w["plew"], w["wg"], w["wp"], fw,
                    tmo, final)


def _pick_tiles(T):
    return (min(1024, T), min(512, T), min(1024, T), min(256, T), min(1024, T))


def kernel(x_prompt, x_sample, p_prompt, p_sample, mix_norm, w_in, q_norm, k_norm, w_gate_up_fwd,
           b_gate_fwd, w_gate_up_bwd, b_gate_bwd, gla_norm, w_out, ple_norm, w_ple_gate, w_ple_proj,
           final_norm):
    depth = w_in.shape[0]
    fw = final_norm[None]
    outs = []
    for x, p in ((x_prompt, p_prompt), (x_sample, p_sample)):
        B, T, _ = x.shape
        h = x.reshape(B * T, D_MODEL)
        for i in range(depth):
            w = _prep_layer_weights(mix_norm[i], w_in[i], q_norm[i], k_norm[i], w_gate_up_fwd[i],
                                    b_gate_fwd[i], w_gate_up_bwd[i], b_gate_bwd[i], gla_norm[i], w_out[i],
                                    ple_norm[i], w_ple_gate[i], w_ple_proj[i])
            h = _layer(h, p[i].reshape(B * T, PLE_DIM), B, T, w, fw, i == depth - 1, _pick_tiles(T))
        outs.append(h.reshape(B, T, D_MODEL))
    return tuple(outs)
```

```python
import functools
import math

import jax
import jax.numpy as jnp
from jax import lax
from jax.experimental import pallas as pl
from jax.experimental.pallas import tpu as pltpu

F32 = jnp.float32
BF16 = jnp.bfloat16

D_MODEL = 1024
GRID_W = 64
N_Q_HEADS = 8
N_KV_HEADS = 2
HEAD_DIM = 64
Q_PER_KV = N_Q_HEADS // N_KV_HEADS
ATTN_WIDTH = N_Q_HEADS * HEAD_DIM
KV_WIDTH = N_KV_HEADS * HEAD_DIM
ROPE_THETA = 10000.0
GLA_HEADS = 4
GLA_DK = 64
GLA_DV = 128
GLA_KEY_WIDTH = GLA_HEADS * GLA_DK
GLA_WIDTH = GLA_HEADS * GLA_DV
GATE_RANK = 16
GATE_NORMALIZER = 16.0
CHUNK = 64
PLE_DIM = 256
EPS = 1e-6
LOG2E = math.log2(math.e)

VMEM_LIMIT_BYTES = 56 * 1024 * 1024
BF16_SUBLANES = 16
V_ROWS = HEAD_DIM + BF16_SUBLANES
NEG_BIG = -1e30
Q_COLS = 512
KEY_ROWS = 256
LOOP_STEPS = 4
SHIFT_HEADROOM = 100.0
MIN_DENOMINATOR = 2.0 ** -60
TOKEN_SUB = 256
GLA_SUB_BLOCKS = 4

NT_DIMS = (((1,), (1,)), ((), ()))


def _dot(a, b):
    return jnp.dot(a, b, preferred_element_type=F32)


def _silu(x):
    return x * jax.nn.sigmoid(x)


def _compiler_params(grid_rank):
    return pltpu.CompilerParams(dimension_semantics=("arbitrary",) * grid_rank,
                                vmem_limit_bytes=VMEM_LIMIT_BYTES)


def _inproj_kernel(x_ref, mixw_ref, wrow_ref, wt_ref, nw_ref, rope_ref, wup_ref, bup_ref,
                   qT_ref, k_ref, vT_ref, ag_ref, lq_ref, lv_ref, lg_ref, lkT_ref, gT_ref):
    def norm_rope(zh, w, ts):
        cr, sr, cc, sc = rope_ref[0, :, ts], rope_ref[1, :, ts], rope_ref[2, :, ts], rope_ref[3, :, ts]
        y = zh * lax.rsqrt(jnp.mean(zh * zh, axis=0, keepdims=True) + EPS) * w
        r1, r2, c1, c2 = y[0:16], y[16:32], y[32:48], y[48:64]
        return jnp.concatenate([r1 * cr - r2 * sr, r2 * cr + r1 * sr,
                                c1 * cc - c2 * sc, c2 * cc + c1 * sc], axis=0)

    def feature_major_epilogue(ts, zt):
        for h in range(N_Q_HEADS):
            qh = norm_rope(zt[h * HEAD_DIM:(h + 1) * HEAD_DIM], nw_ref[h], ts)
            qT_ref[0, h * HEAD_DIM:(h + 1) * HEAD_DIM, ts] = qh.astype(BF16)
        kT = jnp.concatenate(
            [norm_rope(zt[ATTN_WIDTH + h * HEAD_DIM:ATTN_WIDTH + (h + 1) * HEAD_DIM], nw_ref[N_Q_HEADS + h], ts)
             for h in range(N_KV_HEADS)], axis=0)
        k_ref[ts, :] = kT.T.astype(BF16)
        ones = jnp.ones((BF16_SUBLANES, TOKEN_SUB), F32)
        v0 = ATTN_WIDTH + KV_WIDTH
        vT = jnp.concatenate([zt[v0:v0 + HEAD_DIM], ones, zt[v0 + HEAD_DIM:v0 + 2 * HEAD_DIM], ones], axis=0)
        vT_ref[0, 0, :, ts] = vT.astype(BF16)
        l0 = v0 + KV_WIDTH
        lkT_ref[0, :, ts] = zt[l0:l0 + GLA_KEY_WIDTH]
        lrT = zt[l0 + GLA_KEY_WIDTH:l0 + GLA_KEY_WIDTH + 2 * GATE_RANK].astype(BF16)
        gpre = _dot(wup_ref[...], lrT) + bup_ref[...]
        gT_ref[0, :, ts] = jax.nn.log_sigmoid(gpre) * (1.0 / GATE_NORMALIZER)

    subs = [slice(j * TOKEN_SUB, (j + 1) * TOKEN_SUB) for j in range(x_ref.shape[0] // TOKEN_SUB)]
    xns, zts = [], []
    for j, ts in enumerate(subs):
        x = x_ref[ts, :]
        xn = (x * lax.rsqrt(jnp.mean(x * x, axis=-1, keepdims=True) + EPS) * mixw_ref[...]).astype(BF16)
        xns.append(xn)
        zts.append(lax.dot_general(wt_ref[...], xn, NT_DIMS, preferred_element_type=F32))
        if j > 0:
            feature_major_epilogue(subs[j - 1], zts[j - 1])
    xn = jnp.concatenate(xns, axis=0)

    groups = [(0, 512, ag_ref, _silu), (768, 1280, lv_ref, None), (1280, 1792, lg_ref, _silu),
              (512, 768, lq_ref, None)]
    token_major = []

    def store(gi):
        _, _, out_ref, act = groups[gi]
        out_ref[...] = token_major[gi] if act is None else act(token_major[gi])

    for gi, (c0, c1, _, _) in enumerate(groups):
        token_major.append(_dot(xn, wrow_ref[:, c0:c1]))
        if gi == 0:
            feature_major_epilogue(subs[-1], zts[-1])
        else:
            store(gi - 1)
    store(len(groups) - 1)


def _inproj(x2d, B, T, tm, kc, mixw, wrow, wt, nw, rope, wup, bup):
    nt = T // tm
    per_kc = kc // tm
    grid = (B, nt)
    tok = lambda b, i: (b * nt + i, 0)
    fm = lambda b, i: (b, 0, i)
    const2 = lambda b, i: (0, 0)
    const3 = lambda b, i: (0, 0, 0)
    out_shape = (
        jax.ShapeDtypeStruct((B, ATTN_WIDTH, T), BF16),
        jax.ShapeDtypeStruct((B * T, KV_WIDTH), BF16),
        jax.ShapeDtypeStruct((B, T // kc, N_KV_HEADS * V_ROWS, kc), BF16),
        jax.ShapeDtypeStruct((B * T, ATTN_WIDTH), F32),
        jax.ShapeDtypeStruct((B * T, GLA_KEY_WIDTH), F32),
        jax.ShapeDtypeStruct((B * T, GLA_WIDTH), F32),
        jax.ShapeDtypeStruct((B * T, GLA_WIDTH), F32),
        jax.ShapeDtypeStruct((B, GLA_KEY_WIDTH, T), F32),
        jax.ShapeDtypeStruct((B, 2 * GLA_KEY_WIDTH, T), F32),
    )
    out_specs = (
        pl.BlockSpec((1, ATTN_WIDTH, tm), fm),
        pl.BlockSpec((tm, KV_WIDTH), tok),
        pl.BlockSpec((1, 1, N_KV_HEADS * V_ROWS, tm), lambda b, i: (b, i // per_kc, 0, i % per_kc)),
        pl.BlockSpec((tm, ATTN_WIDTH), tok),
        pl.BlockSpec((tm, GLA_KEY_WIDTH), tok),
        pl.BlockSpec((tm, GLA_WIDTH), tok),
        pl.BlockSpec((tm, GLA_WIDTH), tok),
        pl.BlockSpec((1, GLA_KEY_WIDTH, tm), fm),
        pl.BlockSpec((1, 2 * GLA_KEY_WIDTH, tm), fm),
    )
    in_specs = [
        pl.BlockSpec((tm, D_MODEL), tok),
        pl.BlockSpec(mixw.shape, const2),
        pl.BlockSpec(wrow.shape, const2),
        pl.BlockSpec(wt.shape, const2),
        pl.BlockSpec(nw.shape, const3),
        pl.BlockSpec((4, HEAD_DIM // 4, tm), lambda b, i: (0, 0, i)),
        pl.BlockSpec(wup.shape, const2),
        pl.BlockSpec(bup.shape, const2),
    ]
    return pl.pallas_call(
        _inproj_kernel, grid=grid, in_specs=in_specs, out_specs=out_specs, out_shape=out_shape,
        name="inproj",
        compiler_params=_compiler_params(2),
    )(x2d, mixw, wrow, wt, nw, rope, wup, bup)


def _attn_kernel(qT_ref, k_ref, vT_ref, gate_ref, kb_ref, o_ref, w_ref, m_ref, acc_ref, sa_ref, sb_ref, fin_ref,
                 *, tq, kc, nchunks, ntiles):
    assert nchunks % 2 == 0
    g = pl.program_id(1)
    i = pl.program_id(2)
    nq = Q_PER_KV * tq
    cur = i % 2
    prv = 1 - cur
    last = nchunks - 1

    @pl.when(i < ntiles)
    def _():
        qcat = jnp.concatenate([qT_ref[0, h * HEAD_DIM:(h + 1) * HEAD_DIM, :] for h in range(Q_PER_KV)], axis=1)
        zeros = jnp.zeros_like(qcat)
        w_ref[cur] = jnp.where(g == 0, jnp.concatenate([qcat, zeros], axis=0),
                               jnp.concatenate([zeros, qcat], axis=0))
        qf = qcat.astype(F32)
        m_ref[cur] = jnp.sqrt(jnp.sum(qf * qf, axis=0, keepdims=True)) * kb_ref[...] - SHIFT_HEADROOM
        acc_ref[cur] = jnp.zeros((V_ROWS, nq), F32)

    def key_chunk(c):
        start = c * kc if isinstance(c, int) else pl.multiple_of(c * kc, kc)
        return k_ref[pl.ds(start, kc), :]

    def step(c_cur, slot_cur, s_cur, c_nxt, slot_nxt, s_nxt):
        kch = None if s_nxt is None else key_chunk(c_nxt)
        vch = None if s_cur is None else vT_ref[0, c_cur]
        for j in range(nq // Q_COLS):
            cols = slice(j * Q_COLS, (j + 1) * Q_COLS)
            for r in range(kc // KEY_ROWS):
                rows = slice(r * KEY_ROWS, (r + 1) * KEY_ROWS)
                if s_nxt is not None:
                    s_nxt[rows, cols] = _dot(kch[rows], w_ref[slot_nxt, :, cols])
                if s_cur is not None:
                    p = jnp.exp2(s_cur[rows, cols] - m_ref[slot_cur, :, cols]).astype(BF16)
                    acc_ref[slot_cur, :, cols] += _dot(vch[:, rows], p)

    @pl.when(i == 0)
    def _():
        step(None, None, None, 0, cur, sa_ref)

    @pl.when(jnp.logical_and(i > 0, i < ntiles))
    def _():
        step(last, prv, sb_ref, 0, cur, sa_ref)

    @pl.when(i == ntiles)
    def _():
        step(last, prv, sb_ref, None, None, None)

    @pl.when(i > 0)
    def _():
        @pl.when(jnp.logical_not(jnp.min(acc_ref[prv, HEAD_DIM:HEAD_DIM + 1, :]) >= MIN_DENOMINATOR))
        def _():
            m_ref[prv] = jnp.full((1, nq), NEG_BIG, F32)
            acc_ref[prv] = jnp.zeros((V_ROWS, nq), F32)

            def running_max_body(c, carry):
                s = _dot(key_chunk(c), w_ref[prv])
                m_old = m_ref[prv]
                m_new = jnp.maximum(m_old, jnp.max(s, axis=0, keepdims=True))
                p = jnp.exp2(s - m_new).astype(BF16)
                acc_ref[prv] = acc_ref[prv] * jnp.exp2(m_old - m_new) + _dot(vT_ref[0, c], p)
                m_ref[prv] = m_new
                return carry

            lax.fori_loop(0, nchunks, running_max_body, 0)

        fin_ref[...] = acc_ref[prv, 0:HEAD_DIM + 8]

    @pl.when(i == 0)
    def _():
        fin_ref[...] = jnp.ones(fin_ref.shape, F32)

    def finish():
        o = fin_ref[0:HEAD_DIM] / fin_ref[HEAD_DIM:HEAD_DIM + 1]
        oT = jnp.concatenate([o[:, h * tq:(h + 1) * tq] for h in range(Q_PER_KV)], axis=0)
        o_ref[...] = (oT.T * gate_ref[...]).astype(BF16)

    @pl.when(i == ntiles)
    def _():
        finish()

    @pl.when(i < ntiles)
    def _():
        def steps(c0, count):
            for d in range(count):
                bufs = (sa_ref, sb_ref) if d % 2 == 0 else (sb_ref, sa_ref)
                step(c0 + d, cur, bufs[0], c0 + d + 1, cur, bufs[1])

        def loop_body(it, carry):
            steps(LOOP_STEPS * it, LOOP_STEPS)
            return carry

        trips = (nchunks - 1) // LOOP_STEPS
        if trips > 0:
            lax.fori_loop(0, trips, loop_body, 0)
        finish()
        steps(trips * LOOP_STEPS, nchunks - 1 - trips * LOOP_STEPS)


def _attention(qT, k, vT, gate, kbound, B, T, tq, kc):
    ntiles = T // tq
    nchunks = T // kc
    gw = Q_PER_KV * HEAD_DIM
    nq = Q_PER_KV * tq
    kern = functools.partial(_attn_kernel, tq=tq, kc=kc, nchunks=nchunks, ntiles=ntiles)
    done = lambda b, g, i: (b * ntiles + jnp.maximum(i - 1, 0), g)
    return pl.pallas_call(
        kern,
        grid=(B, N_KV_HEADS, ntiles + 1),
        in_specs=[
            pl.BlockSpec((1, gw, tq), lambda b, g, i: (b, g, jnp.minimum(i, ntiles - 1))),
            pl.BlockSpec((T, KV_WIDTH), lambda b, g, i: (b, 0)),
            pl.BlockSpec((1, nchunks, V_ROWS, kc), lambda b, g, i: (b, 0, g, 0)),
            pl.BlockSpec((tq, gw), done),
            pl.BlockSpec((1, 1), lambda b, g, i: (0, 0)),
        ],
        out_specs=pl.BlockSpec((tq, gw), done),
        out_shape=jax.ShapeDtypeStruct((B * T, ATTN_WIDTH), BF16),
        scratch_shapes=[pltpu.VMEM((2, KV_WIDTH, nq), BF16), pltpu.VMEM((2, 1, nq), F32),
                        pltpu.VMEM((2, V_ROWS, nq), F32), pltpu.VMEM((kc, nq), F32), pltpu.VMEM((kc, nq), F32),
                        pltpu.VMEM((HEAD_DIM + 8, nq), F32)],
        name="attention",
        compiler_params=_compiler_params(3),
    )(qT, k, vT, gate, kbound)


def _gla_chains(chains, lg):
    nch = lg // CHUNK
    t_src = lax.broadcasted_iota(jnp.int32, (lg, lg), 0)
    t_dst = lax.broadcasted_iota(jnp.int32, (lg, lg), 1)
    same = (t_src // CHUNK) == (t_dst // CHUNK)
    lane = lax.broadcasted_iota(jnp.int32, (GLA_KEY_WIDTH, lg), 1)
    heads = [(slice(h * GLA_DK, (h + 1) * GLA_DK), slice(h * GLA_DV, (h + 1) * GLA_DV)) for h in range(GLA_HEADS)]

    bTs = []
    for (_, _, _, gT, _, reverse) in chains:
        tri = jnp.where(same & ((t_src >= t_dst) if reverse else (t_src <= t_dst)), 1.0, 0.0).astype(BF16)
        g1 = gT.astype(BF16)
        r1 = gT - g1.astype(F32)
        g2 = r1.astype(BF16)
        g3 = (r1 - g2.astype(F32)).astype(BF16)
        bTs.append(_dot(g1, tri) + _dot(g2, tri) + _dot(g3, tri))

    prep = []
    for (lq, lv, kT, _, _, reverse), bT in zip(chains, bTs):
        tot_cols = []
        btot = jnp.zeros_like(bT)
        for n in range(nch):
            edge = n * CHUNK if reverse else n * CHUNK + CHUNK - 1
            col = bT[:, edge:edge + 1]
            tot_cols.append(jnp.exp(col))
            btot = jnp.where((lane // CHUNK) == n, col, btot)
        k_decT = (kT * jnp.exp(-bT)).astype(BF16)
        k_kvT = (kT * jnp.exp(btot - bT)).astype(BF16)
        q_dec = (lq * (GLA_DK ** -0.5) * jnp.exp(bT.T)).astype(BF16)
        v = lv.astype(BF16)
        intra = same & ((t_src < t_dst) if reverse else (t_src >= t_dst))
        prep.append((q_dec, k_decT, k_kvT, v, tot_cols, intra))

    a = [[jnp.where(intra, _dot(q_dec[:, ks], k_decT[ks, :]), 0.0).astype(BF16) for ks, _ in heads]
         for (q_dec, k_decT, _, _, _, intra) in prep]
    kv = [[[_dot(k_kvT[ks, n * CHUNK:(n + 1) * CHUNK], v[n * CHUNK:(n + 1) * CHUNK, vs]) for n in range(nch)]
           for ks, vs in heads] for (_, _, k_kvT, v, _, _) in prep]
    o = [[_dot(a[ci][h], v[:, vs]) for h, (_, vs) in enumerate(heads)]
         for ci, (_, _, _, v, _, _) in enumerate(prep)]

    outs = []
    for ci, ((_, _, _, _, s_ref, reverse), (q_dec, _, _, _, tot_cols, _)) in enumerate(zip(chains, prep)):
        order = range(nch - 1, -1, -1) if reverse else range(nch)
        per_head = []
        for h, (ks, _) in enumerate(heads):
            inter = [None] * nch
            state = s_ref[h]
            for n in order:
                inter[n] = _dot(q_dec[n * CHUNK:(n + 1) * CHUNK, ks], state.astype(BF16))
                state = tot_cols[n][ks] * state + kv[ci][h][n]
            s_ref[h] = state
            per_head.append(o[ci][h] + jnp.concatenate(inter, axis=0))
        outs.append(jnp.concatenate(per_head, axis=1))
    return outs


def _gla_kernel(lq_f, lv_f, lkT_f, gT_f, lq_b, lv_b, lkT_b, gT_b, of_ref, ob_ref, sf_ref, sb_ref, *, lg, nsub):
    @pl.when(pl.program_id(1) == 0)
    def _():
        sf_ref[...] = jnp.zeros_like(sf_ref)
        sb_ref[...] = jnp.zeros_like(sb_ref)

    def sub(refs, j, s_ref, reverse):
        lq_ref, lv_ref, lkT_ref, gT_ref = refs
        ts = slice(j * lg, (j + 1) * lg)
        return (lq_ref[ts, :], lv_ref[ts, :], lkT_ref[0, :, ts], gT_ref[0, :, ts], s_ref, reverse)

    fwd = [sub((lq_f, lv_f, lkT_f, gT_f), j, sf_ref, False) for j in range(nsub)]
    bwd = [sub((lq_b, lv_b, lkT_b, gT_b), j, sb_ref, True) for j in range(nsub - 1, -1, -1)]
    outs = _gla_chains(fwd + bwd, lg)
    for j in range(nsub):
        of_ref[j * lg:(j + 1) * lg, :] = outs[j]
        ob_ref[(nsub - 1 - j) * lg:(nsub - j) * lg, :] = outs[nsub + j]


def _gla(lq, lv, lkT, gT, B, T, lg, nsub):
    blk = lg * nsub
    nb = T // blk
    ftok = lambda b, i: (b * nb + i, 0)
    rtok = lambda b, i: (b * nb + (nb - 1 - i), 0)
    state = pltpu.VMEM((GLA_HEADS, GLA_DK, GLA_DV), F32)
    o_shape = jax.ShapeDtypeStruct((B * T, GLA_WIDTH), F32)
    return pl.pallas_call(
        functools.partial(_gla_kernel, lg=lg, nsub=nsub),
        grid=(B, nb),
        in_specs=[
            pl.BlockSpec((blk, GLA_KEY_WIDTH), ftok),
            pl.BlockSpec((blk, GLA_WIDTH), ftok),
            pl.BlockSpec((1, GLA_KEY_WIDTH, blk), lambda b, i: (b, 0, i)),
            pl.BlockSpec((1, GLA_KEY_WIDTH, blk), lambda b, i: (b, 0, i)),
            pl.BlockSpec((blk, GLA_KEY_WIDTH), rtok),
            pl.BlockSpec((blk, GLA_WIDTH), rtok),
            pl.BlockSpec((1, GLA_KEY_WIDTH, blk), lambda b, i: (b, 0, nb - 1 - i)),
            pl.BlockSpec((1, GLA_KEY_WIDTH, blk), lambda b, i: (b, 1, nb - 1 - i)),
        ],
        out_specs=(pl.BlockSpec((blk, GLA_WIDTH), ftok), pl.BlockSpec((blk, GLA_WIDTH), rtok)),
        out_shape=(o_shape, o_shape),
        scratch_shapes=[state, state], name="gla",
        compiler_params=_compiler_params(2),
    )(lq, lv, lkT, gT, lq, lv, lkT, gT)


def _out_kernel(x_ref, ma_ref, of_ref, ob_ref, lgate_ref, glaw_ref, p_ref, wo_ref, plew_ref, wg_ref, wp_ref,
                fw_ref, y_ref, *, final):
    subs = [slice(j * TOKEN_SUB, (j + 1) * TOKEN_SUB) for j in range(x_ref.shape[0] // TOKEN_SUB)]
    ha = [x_ref[ts, :] + _dot(ma_ref[ts, :], wo_ref[0:ATTN_WIDTH, :]) for ts in subs]
    pe = [_dot(p_ref[ts, :].astype(BF16), wp_ref[...]) for ts in subs]
    hs = []
    for ts, h in zip(subs, ha):
        o = of_ref[ts, :] + ob_ref[ts, :]
        lgate = lgate_ref[ts, :]
        mg = []
        for hd in range(GLA_HEADS):
            vs = slice(hd * GLA_DV, (hd + 1) * GLA_DV)
            oh = o[:, vs]
            y = oh * lax.rsqrt(jnp.mean(oh * oh, axis=-1, keepdims=True) + EPS) * glaw_ref[...]
            mg.append((y * lgate[:, vs]).astype(BF16))
        hs.append(h + _dot(jnp.concatenate(mg, axis=1), wo_ref[ATTN_WIDTH:, :]))
    gates = []
    for h in hs:
        hn = h * lax.rsqrt(jnp.mean(h * h, axis=-1, keepdims=True) + EPS) * plew_ref[...]
        gates.append(_dot(hn.astype(BF16), wg_ref[...]))
    for ts, h, g, e in zip(subs, hs, gates, pe):
        h = h + jax.nn.sigmoid(g) * e
        if final:
            h = h * lax.rsqrt(jnp.mean(h * h, axis=-1, keepdims=True) + EPS) * fw_ref[...]
        y_ref[ts, :] = h


def _outproj(x2d, ma, o_fwd, o_bwd, lgate, glaw, p2d, wo, plew, wg, wp, fw, tm, final):
    n = x2d.shape[0]
    tok = lambda i: (i, 0)
    const = lambda i: (0, 0)
    return pl.pallas_call(
        functools.partial(_out_kernel, final=final),
        grid=(n // tm,),
        in_specs=[
            pl.BlockSpec((tm, D_MODEL), tok),
            pl.BlockSpec((tm, ATTN_WIDTH), tok),
            pl.BlockSpec((tm, GLA_WIDTH), tok),
            pl.BlockSpec((tm, GLA_WIDTH), tok),
            pl.BlockSpec((tm, GLA_WIDTH), tok),
            pl.BlockSpec(glaw.shape, const),
            pl.BlockSpec((tm, PLE_DIM), tok),
            pl.BlockSpec(wo.shape, const),
            pl.BlockSpec(plew.shape, const),
            pl.BlockSpec(wg.shape, const),
            pl.BlockSpec(wp.shape, const),
            pl.BlockSpec(fw.shape, const),
        ],
        out_specs=pl.BlockSpec((tm, D_MODEL), tok),
        out_shape=jax.ShapeDtypeStruct((n, D_MODEL), F32),
        name="outproj",
        compiler_params=_compiler_params(1),
    )(x2d, ma, o_fwd, o_bwd, lgate, glaw, p2d, wo, plew, wg, wp, fw)


def _rope_tables(T):
    rows = T // GRID_W
    half = HEAD_DIM // 2
    inv_freq = ROPE_THETA ** (-jnp.arange(0, half, 2, dtype=F32) / half)
    ang_r = inv_freq[:, None] * jnp.arange(rows, dtype=F32)[None, :]
    ang_c = inv_freq[:, None] * jnp.arange(GRID_W, dtype=F32)[None, :]
    by_row = lambda a: jnp.repeat(a, GRID_W, axis=1)
    by_col = lambda a: jnp.tile(a, (1, rows))
    return jnp.stack([by_row(jnp.cos(ang_r)), by_row(jnp.sin(ang_r)),
                      by_col(jnp.cos(ang_c)), by_col(jnp.sin(ang_c))])


def _prep_layer_weights(mix_norm, w_in, q_norm, k_norm, w_up_f, b_f, w_up_b, b_b, gla_norm, w_out,
                        ple_norm, w_ple_gate, w_ple_proj):
    c = [0, 512, 640, 768, 1280, 1536, 1792, 2304, 2816, 2832, 2848]
    seg = lambda j: w_in[:, c[j]:c[j + 1]]
    a_q, a_k, a_v, a_gate, l_q, l_k, l_v, l_gate, lr_f, lr_b = (seg(j) for j in range(10))
    wrow = jnp.concatenate([a_gate, l_q, l_v, l_gate], axis=1).astype(BF16)
    wt = jnp.concatenate([a_q, a_k, a_v, l_k, lr_f, lr_b], axis=1).T.astype(BF16)
    qs = q_norm * (HEAD_DIM ** -0.5 * LOG2E)
    nw = jnp.concatenate([jnp.tile(qs[None], (N_Q_HEADS, 1)), jnp.tile(k_norm[None], (N_KV_HEADS, 1))])[..., None]
    zero = jnp.zeros((GLA_KEY_WIDTH, GATE_RANK), F32)
    wup = jnp.concatenate([jnp.concatenate([w_up_f.T, zero], axis=1),
                           jnp.concatenate([zero, w_up_b.T], axis=1)], axis=0).astype(BF16)
    bup = jnp.concatenate([b_f, b_b])[:, None]
    kbound = (jnp.max(jnp.abs(k_norm)) * (HEAD_DIM ** 0.5 * (1.0 + 2.0 ** -7))).reshape(1, 1)
    return dict(mixw=mix_norm[None], wrow=wrow, wt=wt, nw=nw, wup=wup, bup=bup, kbound=kbound,
                gla_nw=gla_norm[None],
                wo=w_out.astype(BF16), plew=ple_norm[None], wg=w_ple_gate.astype(BF16),
                wp=w_ple_proj.astype(BF16))


def _layer(h2d, p2d, B, T, w, fw, final, tiles):
    tm, tq, kc, lg, tmo = tiles
    rope = _rope_tables(T)
    qT, k, vT, ag, lq, lv, lgate, lkT, gT = _inproj(
        h2d, B, T, tm, kc, w["mixw"], w["wrow"], w["wt"], w["nw"], rope, w["wup"], w["bup"])
    ma = _attention(qT, k, vT, ag, w["kbound"], B, T, tq, kc)
    o_fwd, o_bwd = _gla(lq, lv, lkT, gT, B, T, lg, GLA_SUB_BLOCKS)
    return _outproj(h2d, ma, o_fwd, o_bwd, lgate, w["gla_nw"], p2d, w["wo"], w["plew"], w["wg"], w["wp"], fw,
                    tmo, final)


def _pick_tiles(T):
    return (min(512, T), min(512, T), min(1024, T), min(256, T), min(1024, T))


def kernel(x_prompt, x_sample, p_prompt, p_sample, mix_norm, w_in, q_norm, k_norm, w_gate_up_fwd,
           b_gate_fwd, w_gate_up_bwd, b_gate_bwd, gla_norm, w_out, ple_norm, w_ple_gate, w_ple_proj,
           final_norm):
    depth = w_in.shape[0]
    fw = final_norm[None]
    outs = []
    for x, p in ((x_prompt, p_prompt), (x_sample, p_sample)):
        B, T, _ = x.shape
        h = x.reshape(B * T, D_MODEL)
        for i in range(depth):
            w = _prep_layer_weights(mix_norm[i], w_in[i], q_norm[i], k_norm[i], w_gate_up_fwd[i],
                                    b_gate_fwd[i], w_gate_up_bwd[i], b_gate_bwd[i], gla_norm[i], w_out[i],
                                    ple_norm[i], w_ple_gate[i], w_ple_proj[i])
            h = _layer(h, p[i].reshape(B * T, PLE_DIM), B, T, w, fw, i == depth - 1, _pick_tiles(T))
        outs.append(h.reshape(B, T, D_MODEL))
    return tuple(outs)
```

```python
import functools
import math

import jax
import jax.numpy as jnp
from jax import lax
from jax.experimental import pallas as pl
from jax.experimental.pallas import tpu as pltpu

F32 = jnp.float32
BF16 = jnp.bfloat16

D_MODEL = 1024
GRID_W = 64
N_Q_HEADS = 8
N_KV_HEADS = 2
HEAD_DIM = 64
Q_PER_KV = N_Q_HEADS // N_KV_HEADS
ATTN_WIDTH = N_Q_HEADS * HEAD_DIM
KV_WIDTH = N_KV_HEADS * HEAD_DIM
ROPE_THETA = 10000.0
GLA_HEADS = 4
GLA_DK = 64
GLA_DV = 128
GLA_KEY_WIDTH = GLA_HEADS * GLA_DK
GLA_WIDTH = GLA_HEADS * GLA_DV
GATE_RANK = 16
GATE_NORMALIZER = 16.0
CHUNK = 64
PLE_DIM = 256
EPS = 1e-6
LOG2E = math.log2(math.e)

VMEM_LIMIT_BYTES = 56 * 1024 * 1024
BF16_SUBLANES = 16
V_ROWS = HEAD_DIM + BF16_SUBLANES
NEG_BIG = -1e30
Q_COLS = 512
KEY_ROWS = 256
LOOP_STEPS = 4
SHIFT_HEADROOM = 100.0
MIN_DENOMINATOR = 2.0 ** -60
TOKEN_SUB = 256
GLA_SUB_BLOCKS = 4

NT_DIMS = (((1,), (1,)), ((), ()))


def _dot(a, b):
    return jnp.dot(a, b, preferred_element_type=F32)


def _silu(x):
    return x * jax.nn.sigmoid(x)


def _compiler_params(grid_rank):
    return pltpu.CompilerParams(dimension_semantics=("arbitrary",) * grid_rank,
                                vmem_limit_bytes=VMEM_LIMIT_BYTES)


def _inproj_kernel(x_ref, mixw_ref, wrow_ref, wt_ref, nw_ref, rope_ref, wup_ref, bup_ref,
                   qT_ref, k_ref, vT_ref, ag_ref, lq_ref, lv_ref, lg_ref, lkT_ref, gT_ref):
    def norm_rope(zh, w, ts):
        cr, sr, cc, sc = rope_ref[0, :, ts], rope_ref[1, :, ts], rope_ref[2, :, ts], rope_ref[3, :, ts]
        y = zh * lax.rsqrt(jnp.mean(zh * zh, axis=0, keepdims=True) + EPS) * w
        r1, r2, c1, c2 = y[0:16], y[16:32], y[32:48], y[48:64]
        return jnp.concatenate([r1 * cr - r2 * sr, r2 * cr + r1 * sr,
                                c1 * cc - c2 * sc, c2 * cc + c1 * sc], axis=0)

    def feature_major_epilogue(ts, zt):
        for h in range(N_Q_HEADS):
            qh = norm_rope(zt[h * HEAD_DIM:(h + 1) * HEAD_DIM], nw_ref[h], ts)
            qT_ref[0, h * HEAD_DIM:(h + 1) * HEAD_DIM, ts] = qh.astype(BF16)
        kT = jnp.concatenate(
            [norm_rope(zt[ATTN_WIDTH + h * HEAD_DIM:ATTN_WIDTH + (h + 1) * HEAD_DIM], nw_ref[N_Q_HEADS + h], ts)
             for h in range(N_KV_HEADS)], axis=0)
        k_ref[ts, :] = kT.T.astype(BF16)
        ones = jnp.ones((BF16_SUBLANES, TOKEN_SUB), F32)
        v0 = ATTN_WIDTH + KV_WIDTH
        vT = jnp.concatenate([zt[v0:v0 + HEAD_DIM], ones, zt[v0 + HEAD_DIM:v0 + 2 * HEAD_DIM], ones], axis=0)
        vT_ref[0, 0, :, ts] = vT.astype(BF16)
        l0 = v0 + KV_WIDTH
        lkT_ref[0, :, ts] = zt[l0:l0 + GLA_KEY_WIDTH]
        lrT = zt[l0 + GLA_KEY_WIDTH:l0 + GLA_KEY_WIDTH + 2 * GATE_RANK].astype(BF16)
        gpre = _dot(wup_ref[...], lrT) + bup_ref[...]
        gT_ref[0, :, ts] = jax.nn.log_sigmoid(gpre) * (1.0 / GATE_NORMALIZER)

    subs = [slice(j * TOKEN_SUB, (j + 1) * TOKEN_SUB) for j in range(x_ref.shape[0] // TOKEN_SUB)]
    xns, zts = [], []
    for j, ts in enumerate(subs):
        x = x_ref[ts, :]
        xn = (x * lax.rsqrt(jnp.mean(x * x, axis=-1, keepdims=True) + EPS) * mixw_ref[...]).astype(BF16)
        xns.append(xn)
        zts.append(lax.dot_general(wt_ref[...], xn, NT_DIMS, preferred_element_type=F32))
        if j > 0:
            feature_major_epilogue(subs[j - 1], zts[j - 1])
    xn = jnp.concatenate(xns, axis=0)

    groups = [(0, 512, ag_ref, _silu), (768, 1280, lv_ref, None), (1280, 1792, lg_ref, _silu),
              (512, 768, lq_ref, None)]
    token_major = []

    def store(gi):
        _, _, out_ref, act = groups[gi]
        out_ref[...] = token_major[gi] if act is None else act(token_major[gi])

    for gi, (c0, c1, _, _) in enumerate(groups):
        token_major.append(_dot(xn, wrow_ref[:, c0:c1]))
        if gi == 0:
            feature_major_epilogue(subs[-1], zts[-1])
        else:
            store(gi - 1)
    store(len(groups) - 1)


def _inproj(x2d, B, T, tm, kc, mixw, wrow, wt, nw, rope, wup, bup):
    nt = T // tm
    per_kc = kc // tm
    grid = (B, nt)
    tok = lambda b, i: (b * nt + i, 0)
    fm = lambda b, i: (b, 0, i)
    const2 = lambda b, i: (0, 0)
    const3 = lambda b, i: (0, 0, 0)
    out_shape = (
        jax.ShapeDtypeStruct((B, ATTN_WIDTH, T), BF16),
        jax.ShapeDtypeStruct((B * T, KV_WIDTH), BF16),
        jax.ShapeDtypeStruct((B, T // kc, N_KV_HEADS * V_ROWS, kc), BF16),
        jax.ShapeDtypeStruct((B * T, ATTN_WIDTH), F32),
        jax.ShapeDtypeStruct((B * T, GLA_KEY_WIDTH), F32),
        jax.ShapeDtypeStruct((B * T, GLA_WIDTH), F32),
        jax.ShapeDtypeStruct((B * T, GLA_WIDTH), F32),
        jax.ShapeDtypeStruct((B, GLA_KEY_WIDTH, T), F32),
        jax.ShapeDtypeStruct((B, 2 * GLA_KEY_WIDTH, T), F32),
    )
    out_specs = (
        pl.BlockSpec((1, ATTN_WIDTH, tm), fm),
        pl.BlockSpec((tm, KV_WIDTH), tok),
        pl.BlockSpec((1, 1, N_KV_HEADS * V_ROWS, tm), lambda b, i: (b, i // per_kc, 0, i % per_kc)),
        pl.BlockSpec((tm, ATTN_WIDTH), tok),
        pl.BlockSpec((tm, GLA_KEY_WIDTH), tok),
        pl.BlockSpec((tm, GLA_WIDTH), tok),
        pl.BlockSpec((tm, GLA_WIDTH), tok),
        pl.BlockSpec((1, GLA_KEY_WIDTH, tm), fm),
        pl.BlockSpec((1, 2 * GLA_KEY_WIDTH, tm), fm),
    )
    in_specs = [
        pl.BlockSpec((tm, D_MODEL), tok),
        pl.BlockSpec(mixw.shape, const2),
        pl.BlockSpec(wrow.shape, const2),
        pl.BlockSpec(wt.shape, const2),
        pl.BlockSpec(nw.shape, const3),
        pl.BlockSpec((4, HEAD_DIM // 4, tm), lambda b, i: (0, 0, i)),
        pl.BlockSpec(wup.shape, const2),
        pl.BlockSpec(bup.shape, const2),
    ]
    return pl.pallas_call(
        _inproj_kernel, grid=grid, in_specs=in_specs, out_specs=out_specs, out_shape=out_shape,
        name="inproj",
        compiler_params=_compiler_params(2),
    )(x2d, mixw, wrow, wt, nw, rope, wup, bup)


def _attn_kernel(qT_ref, k_ref, vT_ref, gate_ref, kb_ref, o_ref, w_ref, m_ref, acc_ref, sa_ref, sb_ref, fin_ref,
                 *, tq, kc, nchunks, ntiles):
    assert nchunks % 2 == 0
    g = pl.program_id(1)
    i = pl.program_id(2)
    nq = Q_PER_KV * tq
    cur = i % 2
    prv = 1 - cur
    last = nchunks - 1

    @pl.when(i < ntiles)
    def _():
        qcat = jnp.concatenate([qT_ref[0, h * HEAD_DIM:(h + 1) * HEAD_DIM, :] for h in range(Q_PER_KV)], axis=1)
        w_ref[cur, pl.ds(pl.multiple_of(g * HEAD_DIM, HEAD_DIM), HEAD_DIM), :] = qcat
        w_ref[cur, pl.ds(pl.multiple_of((1 - g) * HEAD_DIM, HEAD_DIM), HEAD_DIM), :] = jnp.zeros_like(qcat)
        qf = qcat.astype(F32)
        m_ref[cur] = jnp.sqrt(jnp.sum(qf * qf, axis=0, keepdims=True)) * kb_ref[...] - SHIFT_HEADROOM
        acc_ref[cur] = jnp.zeros((V_ROWS, nq), F32)

    def key_chunk(c):
        start = c * kc if isinstance(c, int) else pl.multiple_of(c * kc, kc)
        return k_ref[pl.ds(start, kc), :]

    def step(c_cur, slot_cur, s_cur, c_nxt, slot_nxt, s_nxt):
        kch = None if s_nxt is None else key_chunk(c_nxt)
        vch = None if s_cur is None else vT_ref[0, c_cur]
        for j in range(nq // Q_COLS):
            cols = slice(j * Q_COLS, (j + 1) * Q_COLS)
            for r in range(kc // KEY_ROWS):
                rows = slice(r * KEY_ROWS, (r + 1) * KEY_ROWS)
                if s_nxt is not None:
                    s_nxt[rows, cols] = _dot(kch[rows], w_ref[slot_nxt, :, cols])
                if s_cur is not None:
                    p = jnp.exp2(s_cur[rows, cols] - m_ref[slot_cur, :, cols]).astype(BF16)
                    acc_ref[slot_cur, :, cols] += _dot(vch[:, rows], p)

    @pl.when(i == 0)
    def _():
        step(None, None, None, 0, cur, sa_ref)

    @pl.when(jnp.logical_and(i > 0, i < ntiles))
    def _():
        step(last, prv, sb_ref, 0, cur, sa_ref)

    @pl.when(i == ntiles)
    def _():
        step(last, prv, sb_ref, None, None, None)

    @pl.when(i > 0)
    def _():
        @pl.when(jnp.logical_not(jnp.min(acc_ref[prv, HEAD_DIM:HEAD_DIM + 1, :]) >= MIN_DENOMINATOR))
        def _():
            m_ref[prv] = jnp.full((1, nq), NEG_BIG, F32)
            acc_ref[prv] = jnp.zeros((V_ROWS, nq), F32)

            def running_max_body(c, carry):
                s = _dot(key_chunk(c), w_ref[prv])
                m_old = m_ref[prv]
                m_new = jnp.maximum(m_old, jnp.max(s, axis=0, keepdims=True))
                p = jnp.exp2(s - m_new).astype(BF16)
                acc_ref[prv] = acc_ref[prv] * jnp.exp2(m_old - m_new) + _dot(vT_ref[0, c], p)
                m_ref[prv] = m_new
                return carry

            lax.fori_loop(0, nchunks, running_max_body, 0)

        fin_ref[...] = acc_ref[prv, 0:HEAD_DIM + 8]

    @pl.when(i == 0)
    def _():
        fin_ref[...] = jnp.ones(fin_ref.shape, F32)

    def finish():
        o = fin_ref[0:HEAD_DIM] / fin_ref[HEAD_DIM:HEAD_DIM + 1]
        oT = jnp.concatenate([o[:, h * tq:(h + 1) * tq] for h in range(Q_PER_KV)], axis=0)
        o_ref[...] = (oT.T * gate_ref[...]).astype(BF16)

    @pl.when(i == ntiles)
    def _():
        finish()

    @pl.when(i < ntiles)
    def _():
        def steps(c0, count):
            for d in range(count):
                bufs = (sa_ref, sb_ref) if d % 2 == 0 else (sb_ref, sa_ref)
                step(c0 + d, cur, bufs[0], c0 + d + 1, cur, bufs[1])

        def loop_body(it, carry):
            steps(LOOP_STEPS * it, LOOP_STEPS)
            return carry

        trips = (nchunks - 1) // LOOP_STEPS
        if trips > 0:
            lax.fori_loop(0, trips, loop_body, 0)
        finish()
        steps(trips * LOOP_STEPS, nchunks - 1 - trips * LOOP_STEPS)


def _attention(qT, k, vT, gate, kbound, B, T, tq, kc):
    ntiles = T // tq
    nchunks = T // kc
    gw = Q_PER_KV * HEAD_DIM
    nq = Q_PER_KV * tq
    kern = functools.partial(_attn_kernel, tq=tq, kc=kc, nchunks=nchunks, ntiles=ntiles)
    done = lambda b, g, i: (b * ntiles + jnp.maximum(i - 1, 0), g)
    return pl.pallas_call(
        kern,
        grid=(B, N_KV_HEADS, ntiles + 1),
        in_specs=[
            pl.BlockSpec((1, gw, tq), lambda b, g, i: (b, g, jnp.minimum(i, ntiles - 1))),
            pl.BlockSpec((T, KV_WIDTH), lambda b, g, i: (b, 0)),
            pl.BlockSpec((1, nchunks, V_ROWS, kc), lambda b, g, i: (b, 0, g, 0)),
            pl.BlockSpec((tq, gw), done),
            pl.BlockSpec((1, 1), lambda b, g, i: (0, 0)),
        ],
        out_specs=pl.BlockSpec((tq, gw), done),
        out_shape=jax.ShapeDtypeStruct((B * T, ATTN_WIDTH), BF16),
        scratch_shapes=[pltpu.VMEM((2, KV_WIDTH, nq), BF16), pltpu.VMEM((2, 1, nq), F32),
                        pltpu.VMEM((2, V_ROWS, nq), F32), pltpu.VMEM((kc, nq), F32), pltpu.VMEM((kc, nq), F32),
                        pltpu.VMEM((HEAD_DIM + 8, nq), F32)],
        name="attention",
        compiler_params=_compiler_params(3),
    )(qT, k, vT, gate, kbound)


def _gla_chains(chains, lg):
    nch = lg // CHUNK
    t_src = lax.broadcasted_iota(jnp.int32, (lg, lg), 0)
    t_dst = lax.broadcasted_iota(jnp.int32, (lg, lg), 1)
    same = (t_src // CHUNK) == (t_dst // CHUNK)
    lane = lax.broadcasted_iota(jnp.int32, (GLA_KEY_WIDTH, lg), 1)
    heads = [(slice(h * GLA_DK, (h + 1) * GLA_DK), slice(h * GLA_DV, (h + 1) * GLA_DV)) for h in range(GLA_HEADS)]

    bTs = []
    for (_, _, _, gT, _, reverse) in chains:
        tri = jnp.where(same & ((t_src >= t_dst) if reverse else (t_src <= t_dst)), 1.0, 0.0).astype(BF16)
        g1 = gT.astype(BF16)
        r1 = gT - g1.astype(F32)
        g2 = r1.astype(BF16)
        g3 = (r1 - g2.astype(F32)).astype(BF16)
        bTs.append(_dot(g1, tri) + _dot(g2, tri) + _dot(g3, tri))

    prep = []
    for (lq, lv, kT, _, _, reverse), bT in zip(chains, bTs):
        tot_cols = []
        btot = jnp.zeros_like(bT)
        for n in range(nch):
            edge = n * CHUNK if reverse else n * CHUNK + CHUNK - 1
            col = bT[:, edge:edge + 1]
            tot_cols.append(jnp.exp(col))
            btot = jnp.where((lane // CHUNK) == n, col, btot)
        k_decT = (kT * jnp.exp(-bT)).astype(BF16)
        k_kvT = (kT * jnp.exp(btot - bT)).astype(BF16)
        q_dec = (lq * (GLA_DK ** -0.5) * jnp.exp(bT.T)).astype(BF16)
        v = lv.astype(BF16)
        intra = same & ((t_src < t_dst) if reverse else (t_src >= t_dst))
        prep.append((q_dec, k_decT, k_kvT, v, tot_cols, intra))

    a = [[jnp.where(intra, _dot(q_dec[:, ks], k_decT[ks, :]), 0.0).astype(BF16) for ks, _ in heads]
         for (q_dec, k_decT, _, _, _, intra) in prep]
    kv = [[[_dot(k_kvT[ks, n * CHUNK:(n + 1) * CHUNK], v[n * CHUNK:(n + 1) * CHUNK, vs]) for n in range(nch)]
           for ks, vs in heads] for (_, _, k_kvT, v, _, _) in prep]
    o = [[_dot(a[ci][h], v[:, vs]) for h, (_, vs) in enumerate(heads)]
         for ci, (_, _, _, v, _, _) in enumerate(prep)]

    outs = []
    for ci, ((_, _, _, _, s_ref, reverse), (q_dec, _, _, _, tot_cols, _)) in enumerate(zip(chains, prep)):
        order = range(nch - 1, -1, -1) if reverse else range(nch)
        per_head = []
        for h, (ks, _) in enumerate(heads):
            inter = [None] * nch
            state = s_ref[h]
            for n in order:
                inter[n] = _dot(q_dec[n * CHUNK:(n + 1) * CHUNK, ks], state.astype(BF16))
                state = tot_cols[n][ks] * state + kv[ci][h][n]
            s_ref[h] = state
            per_head.append(o[ci][h] + jnp.concatenate(inter, axis=0))
        outs.append(jnp.concatenate(per_head, axis=1))
    return outs


def _gla_kernel(lq_f, lv_f, lkT_f, gT_f, lq_b, lv_b, lkT_b, gT_b, of_ref, ob_ref, sf_ref, sb_ref, *, lg, nsub):
    @pl.when(pl.program_id(1) == 0)
    def _():
        sf_ref[...] = jnp.zeros_like(sf_ref)
        sb_ref[...] = jnp.zeros_like(sb_ref)

    def sub(refs, j, s_ref, reverse):
        lq_ref, lv_ref, lkT_ref, gT_ref = refs
        ts = slice(j * lg, (j + 1) * lg)
        return (lq_ref[ts, :], lv_ref[ts, :], lkT_ref[0, :, ts], gT_ref[0, :, ts], s_ref, reverse)

    fwd = [sub((lq_f, lv_f, lkT_f, gT_f), j, sf_ref, False) for j in range(nsub)]
    bwd = [sub((lq_b, lv_b, lkT_b, gT_b), j, sb_ref, True) for j in range(nsub - 1, -1, -1)]
    outs = _gla_chains(fwd + bwd, lg)
    for j in range(nsub):
        of_ref[j * lg:(j + 1) * lg, :] = outs[j]
        ob_ref[(nsub - 1 - j) * lg:(nsub - j) * lg, :] = outs[nsub + j]


def _gla(lq, lv, lkT, gT, B, T, lg, nsub):
    blk = lg * nsub
    nb = T // blk
    ftok = lambda b, i: (b * nb + i, 0)
    rtok = lambda b, i: (b * nb + (nb - 1 - i), 0)
    state = pltpu.VMEM((GLA_HEADS, GLA_DK, GLA_DV), F32)
    o_shape = jax.ShapeDtypeStruct((B * T, GLA_WIDTH), F32)
    return pl.pallas_call(
        functools.partial(_gla_kernel, lg=lg, nsub=nsub),
        grid=(B, nb),
        in_specs=[
            pl.BlockSpec((blk, GLA_KEY_WIDTH), ftok),
            pl.BlockSpec((blk, GLA_WIDTH), ftok),
            pl.BlockSpec((1, GLA_KEY_WIDTH, blk), lambda b, i: (b, 0, i)),
            pl.BlockSpec((1, GLA_KEY_WIDTH, blk), lambda b, i: (b, 0, i)),
            pl.BlockSpec((blk, GLA_KEY_WIDTH), rtok),
            pl.BlockSpec((blk, GLA_WIDTH), rtok),
            pl.BlockSpec((1, GLA_KEY_WIDTH, blk), lambda b, i: (b, 0, nb - 1 - i)),
            pl.BlockSpec((1, GLA_KEY_WIDTH, blk), lambda b, i: (b, 1, nb - 1 - i)),
        ],
        out_specs=(pl.BlockSpec((blk, GLA_WIDTH), ftok), pl.BlockSpec((blk, GLA_WIDTH), rtok)),
        out_shape=(o_shape, o_shape),
        scratch_shapes=[state, state], name="gla",
        compiler_params=_compiler_params(2),
    )(lq, lv, lkT, gT, lq, lv, lkT, gT)


def _out_kernel(x_ref, ma_ref, of_ref, ob_ref, lgate_ref, glaw_ref, p_ref, wo_ref, plew_ref, wg_ref, wp_ref,
                fw_ref, y_ref, *, final):
    subs = [slice(j * TOKEN_SUB, (j + 1) * TOKEN_SUB) for j in range(x_ref.shape[0] // TOKEN_SUB)]
    ha = [x_ref[ts, :] + _dot(ma_ref[ts, :], wo_ref[0:ATTN_WIDTH, :]) for ts in subs]
    pe = [_dot(p_ref[ts, :].astype(BF16), wp_ref[...]) for ts in subs]
    hs = []
    for ts, h in zip(subs, ha):
        o = of_ref[ts, :] + ob_ref[ts, :]
        lgate = lgate_ref[ts, :]
        mg = []
        for hd in range(GLA_HEADS):
            vs = slice(hd * GLA_DV, (hd + 1) * GLA_DV)
            oh = o[:, vs]
            y = oh * lax.rsqrt(jnp.mean(oh * oh, axis=-1, keepdims=True) + EPS) * glaw_ref[...]
            mg.append((y * lgate[:, vs]).astype(BF16))
        hs.append(h + _dot(jnp.concatenate(mg, axis=1), wo_ref[ATTN_WIDTH:, :]))
    gates = []
    for h in hs:
        hn = h * lax.rsqrt(jnp.mean(h * h, axis=-1, keepdims=True) + EPS) * plew_ref[...]
        gates.append(_dot(hn.astype(BF16), wg_ref[...]))
    for ts, h, g, e in zip(subs, hs, gates, pe):
        h = h + jax.nn.sigmoid(g) * e
        if final:
            h = h * lax.rsqrt(jnp.mean(h * h, axis=-1, keepdims=True) + EPS) * fw_ref[...]
        y_ref[ts, :] = h


def _outproj(x2d, ma, o_fwd, o_bwd, lgate, glaw, p2d, wo, plew, wg, wp, fw, tm, final):
    n = x2d.shape[0]
    tok = lambda i: (i, 0)
    const = lambda i: (0, 0)
    return pl.pallas_call(
        functools.partial(_out_kernel, final=final),
        grid=(n // tm,),
        in_specs=[
            pl.BlockSpec((tm, D_MODEL), tok),
            pl.BlockSpec((tm, ATTN_WIDTH), tok),
            pl.BlockSpec((tm, GLA_WIDTH), tok),
            pl.BlockSpec((tm, GLA_WIDTH), tok),
            pl.BlockSpec((tm, GLA_WIDTH), tok),
            pl.BlockSpec(glaw.shape, const),
            pl.BlockSpec((tm, PLE_DIM), tok),
            pl.BlockSpec(wo.shape, const),
            pl.BlockSpec(plew.shape, const),
            pl.BlockSpec(wg.shape, const),
            pl.BlockSpec(wp.shape, const),
            pl.BlockSpec(fw.shape, const),
        ],
        out_specs=pl.BlockSpec((tm, D_MODEL), tok),
        out_shape=jax.ShapeDtypeStruct((n, D_MODEL), F32),
        name="outproj",
        compiler_params=_compiler_params(1),
    )(x2d, ma, o_fwd, o_bwd, lgate, glaw, p2d, wo, plew, wg, wp, fw)


def _rope_tables(T):
    rows = T // GRID_W
    half = HEAD_DIM // 2
    inv_freq = ROPE_THETA ** (-jnp.arange(0, half, 2, dtype=F32) / half)
    ang_r = inv_freq[:, None] * jnp.arange(rows, dtype=F32)[None, :]
    ang_c = inv_freq[:, None] * jnp.arange(GRID_W, dtype=F32)[None, :]
    by_row = lambda a: jnp.repeat(a, GRID_W, axis=1)
    by_col = lambda a: jnp.tile(a, (1, rows))
    return jnp.stack([by_row(jnp.cos(ang_r)), by_row(jnp.sin(ang_r)),
                      by_col(jnp.cos(ang_c)), by_col(jnp.sin(ang_c))])


def _prep_layer_weights(mix_norm, w_in, q_norm, k_norm, w_up_f, b_f, w_up_b, b_b, gla_norm, w_out,
                        ple_norm, w_ple_gate, w_ple_proj):
    c = [0, 512, 640, 768, 1280, 1536, 1792, 2304, 2816, 2832, 2848]
    seg = lambda j: w_in[:, c[j]:c[j + 1]]
    a_q, a_k, a_v, a_gate, l_q, l_k, l_v, l_gate, lr_f, lr_b = (seg(j) for j in range(10))
    wrow = jnp.concatenate([a_gate, l_q, l_v, l_gate], axis=1).astype(BF16)
    wt = jnp.concatenate([a_q, a_k, a_v, l_k, lr_f, lr_b], axis=1).T.astype(BF16)
    qs = q_norm * (HEAD_DIM ** -0.5 * LOG2E)
    nw = jnp.concatenate([jnp.tile(qs[None], (N_Q_HEADS, 1)), jnp.tile(k_norm[None], (N_KV_HEADS, 1))])[..., None]
    zero = jnp.zeros((GLA_KEY_WIDTH, GATE_RANK), F32)
    wup = jnp.concatenate([jnp.concatenate([w_up_f.T, zero], axis=1),
                           jnp.concatenate([zero, w_up_b.T], axis=1)], axis=0).astype(BF16)
    bup = jnp.concatenate([b_f, b_b])[:, None]
    kbound = (jnp.max(jnp.abs(k_norm)) * (HEAD_DIM ** 0.5 * (1.0 + 2.0 ** -7))).reshape(1, 1)
    return dict(mixw=mix_norm[None], wrow=wrow, wt=wt, nw=nw, wup=wup, bup=bup, kbound=kbound,
                gla_nw=gla_norm[None],
                wo=w_out.astype(BF16), plew=ple_norm[None], wg=w_ple_gate.astype(BF16),
                wp=w_ple_proj.astype(BF16))


def _layer(h2d, p2d, B, T, w, fw, final, tiles):
    tm, tq, kc, lg, tmo = tiles
    rope = _rope_tables(T)
    qT, k, vT, ag, lq, lv, lgate, lkT, gT = _inproj(
        h2d, B, T, tm, kc, w["mixw"], w["wrow"], w["wt"], w["nw"], rope, w["wup"], w["bup"])
    ma = _attention(qT, k, vT, ag, w["kbound"], B, T, tq, kc)
    o_fwd, o_bwd = _gla(lq, lv, lkT, gT, B, T, lg, GLA_SUB_BLOCKS)
    return _outproj(h2d, ma, o_fwd, o_bwd, lgate, w["gla_nw"], p2d, w["wo"], w["plew"], w["wg"], w["wp"], fw,
                    tmo, final)


def _pick_tiles(T):
    return (min(512, T), min(512, T), min(1024, T), min(256, T), min(1024, T))


def kernel(x_prompt, x_sample, p_prompt, p_sample, mix_norm, w_in, q_norm, k_norm, w_gate_up_fwd,
           b_gate_fwd, w_gate_up_bwd, b_gate_bwd, gla_norm, w_out, ple_norm, w_ple_gate, w_ple_proj,
           final_norm):
    depth = w_in.shape[0]
    fw = final_norm[None]
    outs = []
    for x, p in ((x_prompt, p_prompt), (x_sample, p_sample)):
        B, T, _ = x.shape
        h = x.reshape(B * T, D_MODEL)
        for i in range(depth):
            w = _prep_layer_weights(mix_norm[i], w_in[i], q_norm[i], k_norm[i], w_gate_up_fwd[i],
                                    b_gate_fwd[i], w_gate_up_bwd[i], b_gate_bwd[i], gla_norm[i], w_out[i],
                                    ple_norm[i], w_ple_gate[i], w_ple_proj[i])
            h = _layer(h, p[i].reshape(B * T, PLE_DIM), B, T, w, fw, i == depth - 1, _pick_tiles(T))
        outs.append(h.reshape(B, T, D_MODEL))
    return tuple(outs)
```
